```python
import math
import jax, jax.numpy as jnp
from jax import lax
import numpy as np

D_MODEL = 4096
BATCH = 32
SEQ = 256
DEPTH = 4
DEC_BATCH = 4
DEC_SEQ = 2048
PAST_LEN = 512

GRID_W = 64
N_MIXERS = 3
N_HYENA_LAYERS = (DEPTH + 2) // 3
N_ATTN_LAYERS = (DEPTH + 1) // 3
N_POOL_LAYERS = DEPTH // 3
N_MOD = 6
N_HEADS = 32
N_KV_HEADS = 8
HEAD_DIM = D_MODEL // N_HEADS
GQA_GROUP = N_HEADS // N_KV_HEADS
QKV_DIM = (N_HEADS + 2 * N_KV_HEADS) * HEAD_DIM
ROPE_AXIS_DIM = HEAD_DIM // 2
ROPE_THETA = 10000.0
Q_BLOCK = 128
RMS_EPS = 1e-6
SHORT_CONV = 3
HYENA_EMB_DIM = 33
HYENA_FILTER_WIDTH = 64
HYENA_FAST_DECAY = 0.3
HYENA_SLOW_DECAY = 1.5
HYENA_DECAY_TARGET = 1e-2
HYENA_DECAY_SHIFT = 0.05
POOL_WINDOWS = (2, 4, 8, 16)
POOL_GROUPS = len(POOL_WINDOWS)
POOL_GROUP_DIM = D_MODEL // POOL_GROUPS
N_EXPERTS = 32
TOP_K = 4
D_EXPERT = D_MODEL // 4
SWIGLU_LIMIT = 7.0
SWIGLU_ALPHA = 1.702
LN_EPS = 1e-5
DEEPNORM_ALPHA = (2 * DEPTH) ** 0.25
DEEPNORM_BETA = (8 * DEPTH) ** -0.25

kernel_name = 'hybrid_diffusion_prefix_trunk_step'


def layer_norm(x, g, b):
    xf = x.astype(jnp.float32)
    mu = jnp.mean(xf, -1, keepdims=True)
    var = jnp.mean(jnp.square(xf - mu), -1, keepdims=True)
    return ((xf - mu) * lax.rsqrt(var + LN_EPS) * g + b).astype(x.dtype)


def rms_norm(x, g):
    xf = x.astype(jnp.float32)
    return (xf * lax.rsqrt(jnp.mean(xf * xf, -1, keepdims=True) + RMS_EPS) * g).astype(x.dtype)


def modulation(cond, w, b):
    m = jax.nn.silu(cond) @ w + b
    return jnp.split(m[..., None, :], N_MOD, axis=-1)


def short_conv(z, w, b):
    y = lax.conv_general_dilated(z, w[:, None, :].astype(z.dtype), window_strides=(1,),
                                 padding=[(SHORT_CONV // 2, SHORT_CONV // 2)],
                                 dimension_numbers=('NWC', 'WIO', 'NWC'),
                                 feature_group_count=z.shape[-1])
    return y + b


def hyena_position_features(L):
    t = jnp.linspace(0.0, 1.0, L, dtype=jnp.float32)[:, None]
    bands = (HYENA_EMB_DIM - 1) // 2
    w = 2.0 * math.pi * jnp.arange(L, dtype=jnp.float32)[:, None] / L
    f = jnp.linspace(1e-4, bands - 1, bands, dtype=jnp.float32)[None, :]
    return t, jnp.concatenate([t, jnp.cos(f * w), -jnp.sin(f * w)], axis=-1)


def hyena_filter(L, w1, w23, b, freq, w_out):
    t, z = hyena_position_features(L)
    h = jnp.sin(freq[0] * (z @ w1 + b[0]))
    h = jnp.sin(freq[1] * (h @ w23[0] + b[1]))
    h = jnp.sin(freq[2] * (h @ w23[1] + b[2]))
    h = (h @ w_out).astype(jnp.float32)
    min_decay = math.log(HYENA_DECAY_TARGET) / HYENA_SLOW_DECAY
    max_decay = math.log(HYENA_DECAY_TARGET) / HYENA_FAST_DECAY
    deltas = jnp.abs(jnp.linspace(min_decay, max_decay, D_MODEL, dtype=jnp.float32))
    window = jnp.exp(-t * jnp.concatenate([deltas, deltas])) + HYENA_DECAY_SHIFT
    h_fwd, h_bwd = jnp.split(h * window, 2, axis=-1)
    k = jnp.concatenate([h_fwd, jnp.zeros((1, D_MODEL), jnp.float32), h_bwd[:0:-1]], axis=0)
    return k / jnp.sum(jnp.abs(k), axis=0, keepdims=True)


def long_conv(v, k, bias):
    L = v.shape[1]
    vf = jnp.fft.rfft(v.astype(jnp.float32), n=2 * L, axis=1)
    kf = jnp.fft.rfft(k, n=2 * L, axis=0)
    y = jnp.fft.irfft(vf * kf[None], n=2 * L, axis=1)[:, :L]
    return (y + v.astype(jnp.float32) * bias).astype(v.dtype)


def hyena_mixer(h, w_in, b_in, conv_w, conv_b, filt, filter_bias, w_out, b_out):
    z = short_conv(h @ w_in + b_in, conv_w, conv_b)
    x0, x1, v = jnp.split(z, 3, axis=-1)
    k = hyena_filter(h.shape[1], *filt)
    v = long_conv(v * x1, k, filter_bias)
    return (v * x0) @ w_out + b_out


def attn_qkv(h, w_qkv, q_norm, k_norm):
    B, L, _ = h.shape
    q, k, v = jnp.split(h @ w_qkv, [N_HEADS * HEAD_DIM, (N_HEADS + N_KV_HEADS) * HEAD_DIM], axis=-1)
    q = rms_norm(q.reshape(B, L, N_HEADS, HEAD_DIM), q_norm)
    k = rms_norm(k.reshape(B, L, N_KV_HEADS, HEAD_DIM), k_norm)
    return q, k, v.reshape(B, L, N_KV_HEADS, HEAD_DIM)


def axial_rope_tables(L):
    rows = L // GRID_W
    row = jnp.repeat(jnp.arange(rows), GRID_W)
    col = jnp.tile(jnp.arange(GRID_W), rows)
    inv_freq = ROPE_THETA ** (-jnp.arange(0, ROPE_AXIS_DIM, 2, dtype=jnp.float32) / ROPE_AXIS_DIM)
    ang = jnp.stack([row, col], axis=-1).astype(jnp.float32)[..., None] * inv_freq
    return jnp.cos(ang), jnp.sin(ang)


def apply_axial_rope(x, cos, sin):
    B, L, H, _ = x.shape
    xr = x.reshape(B, L, H, 2, 2, HEAD_DIM // 4).astype(jnp.float32)
    x1, x2 = xr[..., 0, :], xr[..., 1, :]
    c, s = cos[None, :, None], sin[None, :, None]
    out = jnp.stack([x1 * c - x2 * s, x2 * c + x1 * s], axis=-2)
    return out.reshape(x.shape).astype(x.dtype)


def block_attention(q, k, v):
    B, Lq, H, Dh = q.shape
    nb = Lq // Q_BLOCK
    qb = q.reshape(B, nb, Q_BLOCK, N_KV_HEADS, GQA_GROUP, Dh).transpose(1, 0, 2, 3, 4, 5)
    scale = HEAD_DIM ** -0.5

    def one_block(qi):
        s = jnp.einsum('bqkgd,bskd->bkgqs', qi, k, preferred_element_type=jnp.float32) * scale
        p = jax.nn.softmax(s, axis=-1)
        return jnp.einsum('bkgqs,bskd->bqkgd', p.astype(v.dtype), v)

    o = lax.map(one_block, qb)
    return o.transpose(1, 0, 2, 3, 4, 5).reshape(B, Lq, H * Dh)


def pool_mixer(h, w, b, scale):
    B, L, _ = h.shape
    hg = h.reshape(B, L, POOL_GROUPS, POOL_GROUP_DIM)
    cs = jnp.pad(jnp.cumsum(hg.astype(jnp.float32), axis=1), ((0, 0), (1, 0), (0, 0), (0, 0)))
    win = jnp.array(POOL_WINDOWS, dtype=jnp.int32)[None, :]
    t = jnp.arange(L, dtype=jnp.int32)[:, None]
    lo = jnp.clip(t - win // 2, 0, L)
    hi = jnp.clip(t - win // 2 + win, 0, L)
    gidx = jnp.arange(POOL_GROUPS)[None, :]
    mean = (cs[:, hi, gidx] - cs[:, lo, gidx]) / (hi - lo)[None, :, :, None]
    d = (mean - hg.astype(jnp.float32)).astype(h.dtype)
    y = jnp.einsum('blgc,gce->blge', d, w).reshape(B, L, D_MODEL) + b
    return y * scale


def clamped_swiglu_expert(x, w_gu, b_gu, w_down, b_down):
    g, u = jnp.split(x @ w_gu + b_gu, 2, axis=-1)
    g = jnp.minimum(g, SWIGLU_LIMIT)
    u = jnp.clip(u, -SWIGLU_LIMIT, SWIGLU_LIMIT)
    return (g * jax.nn.sigmoid(SWIGLU_ALPHA * g) * (u + 1.0)) @ w_down + b_down


def moe_ffn(x, w_router, b_router, w_gu, b_gu, w_down, b_down):
    logits = (x @ w_router + b_router).astype(jnp.float32)
    top_val, top_idx = lax.top_k(logits, TOP_K)
    probs = jax.nn.softmax(top_val, axis=-1)
    gates = jnp.einsum('tk,tke->et', probs, jax.nn.one_hot(top_idx, N_EXPERTS, dtype=jnp.float32))

    def add_expert(acc, p):
        w_gu_e, b_gu_e, w_down_e, b_down_e, gate_e = p
        return acc + gate_e[:, None] * clamped_swiglu_expert(x, w_gu_e, b_gu_e, w_down_e, b_down_e), None

    acc, _ = lax.scan(add_expert, jnp.zeros(x.shape, jnp.float32), (w_gu, b_gu, w_down, b_down, gates))
    return acc.astype(x.dtype)


def setup_inputs(seed: int = 0) -> dict:
    key = jax.random.key(seed)
    keys = iter(jax.random.split(key, 48))

    def nrm(shape, scale=1.0):
        return jax.random.normal(next(keys), shape, jnp.float32) * scale

    D, F = D_MODEL, D_EXPERT
    NH, NA, NP = N_HYENA_LAYERS, N_ATTN_LAYERS, N_POOL_LAYERS
    return {
        'x_prompt': nrm((BATCH, SEQ, D)),
        'x_sample': nrm((DEC_BATCH, DEC_SEQ, D)),
        'cache_k': nrm((DEC_BATCH, NA, PAST_LEN, N_KV_HEADS, HEAD_DIM)),
        'cache_v': nrm((DEC_BATCH, NA, PAST_LEN, N_KV_HEADS, HEAD_DIM)),
        'c': nrm((DEC_BATCH, D)),
        'c_ctx': nrm((D,)),
        'mod_w': nrm((DEPTH, D, N_MOD * D), 0.5 * D ** -0.5),
        'mod_b': nrm((DEPTH, N_MOD * D), 0.02),
        'ln_g': 1.0 + nrm((DEPTH, 2, D), 0.02),
        'ln_b': nrm((DEPTH, 2, D), 0.02),
        'hy_w_in': nrm((NH, D, 3 * D), D ** -0.5),
        'hy_b_in': nrm((NH, 3 * D), 0.02),
        'hy_conv_w': nrm((NH, SHORT_CONV, 3 * D), SHORT_CONV ** -0.5),
        'hy_conv_b': nrm((NH, 3 * D), 0.02),
        'hy_ffn_w1': nrm((NH, HYENA_EMB_DIM, HYENA_FILTER_WIDTH), HYENA_EMB_DIM ** -0.5),
        'hy_ffn_w23': nrm((NH, 2, HYENA_FILTER_WIDTH, HYENA_FILTER_WIDTH), HYENA_FILTER_WIDTH ** -0.5),
        'hy_ffn_b': nrm((NH, 3, HYENA_FILTER_WIDTH), 0.1),
        'hy_sin_freq': 1.0 + nrm((NH, 3, HYENA_FILTER_WIDTH), 0.1),
        'hy_ffn_w_out': nrm((NH, HYENA_FILTER_WIDTH, 2 * D), HYENA_FILTER_WIDTH ** -0.5),
        'hy_filter_bias': nrm((NH, D), 0.1),
        'hy_w_out': nrm((NH, D, D), DEEPNORM_BETA * D ** -0.5),
        'hy_b_out': nrm((NH, D), 0.02),
        'at_w_qkv': nrm((NA, D, QKV_DIM), D ** -0.5),
        'at_q_norm': 1.0 + nrm((NA, HEAD_DIM), 0.02),
        'at_k_norm': 1.0 + nrm((NA, HEAD_DIM), 0.02),
        'at_w_o': nrm((NA, N_HEADS * HEAD_DIM, D), DEEPNORM_BETA * (N_HEADS * HEAD_DIM) ** -0.5),
        'pl_w': nrm((NP, POOL_GROUPS, POOL_GROUP_DIM, POOL_GROUP_DIM), DEEPNORM_BETA * POOL_GROUP_DIM ** -0.5),
        'pl_b': nrm((NP, D), 0.02),
        'pl_scale': 1.0 + nrm((NP, D), 0.02),
        'moe_w_router': nrm((DEPTH, D, N_EXPERTS), D ** -0.5),
        'moe_b_router': nrm((DEPTH, N_EXPERTS), 0.01),
        'moe_w_gu': nrm((DEPTH, N_EXPERTS, D, 2 * F), D ** -0.5),
        'moe_b_gu': nrm((DEPTH, N_EXPERTS, 2 * F), 0.01),
        'moe_w_down': nrm((DEPTH, N_EXPERTS, F, D), DEEPNORM_BETA * F ** -0.5),
        'moe_b_down': nrm((DEPTH, N_EXPERTS, D), 0.01),
    }


def reference(x_prompt, x_sample, cache_k, cache_v, c, c_ctx, mod_w, mod_b, ln_g, ln_b,
              hy_w_in, hy_b_in, hy_conv_w, hy_conv_b, hy_ffn_w1, hy_ffn_w23, hy_ffn_b, hy_sin_freq,
              hy_ffn_w_out, hy_filter_bias, hy_w_out, hy_b_out,
              at_w_qkv, at_q_norm, at_k_norm, at_w_o,
              pl_w, pl_b, pl_scale,
              moe_w_router, moe_b_router, moe_w_gu, moe_b_gu, moe_w_down, moe_b_down):
    xp, xs = x_prompt, x_sample
    ctx_k, ctx_v = [], []
    for i in range(DEPTH):
        kind, j = i % N_MIXERS, i // N_MIXERS
        sh_mp, sc_mp, g_mp, sh_fp, sc_fp, g_fp = modulation(c_ctx, mod_w[i], mod_b[i])
        sh_ms, sc_ms, g_ms, sh_fs, sc_fs, g_fs = modulation(c, mod_w[i], mod_b[i])
        hp = xp * (1 + sc_mp) + sh_mp
        hs = xs * (1 + sc_ms) + sh_ms
        if kind == 0:
            hy = (hy_w_in[j], hy_b_in[j], hy_conv_w[j], hy_conv_b[j],
                  (hy_ffn_w1[j], hy_ffn_w23[j], hy_ffn_b[j], hy_sin_freq[j], hy_ffn_w_out[j]),
                  hy_filter_bias[j], hy_w_out[j], hy_b_out[j])
            op = hyena_mixer(hp, *hy)
            os_ = hyena_mixer(hs, *hy)
        elif kind == 1:
            qp, kp, vp = attn_qkv(hp, at_w_qkv[j], at_q_norm[j], at_k_norm[j])
            ctx_k.append(kp)
            ctx_v.append(vp)
            op = block_attention(qp, kp, vp) @ at_w_o[j]
            qs, ks, vs = attn_qkv(hs, at_w_qkv[j], at_q_norm[j], at_k_norm[j])
            cos, sin = axial_rope_tables(hs.shape[1])
            qs = apply_axial_rope(qs, cos, sin)
            ks = jnp.concatenate([cache_k[:, j], apply_axial_rope(ks, cos, sin)], axis=1)
            vs = jnp.concatenate([cache_v[:, j], vs], axis=1)
            os_ = block_attention(qs, ks, vs) @ at_w_o[j]
        else:
            op = pool_mixer(hp, pl_w[j], pl_b[j], pl_scale[j])
            os_ = pool_mixer(hs, pl_w[j], pl_b[j], pl_scale[j])
        xp = layer_norm(DEEPNORM_ALPHA * xp + g_mp * op, ln_g[i, 0], ln_b[i, 0])
        xs = layer_norm(DEEPNORM_ALPHA * xs + g_ms * os_, ln_g[i, 0], ln_b[i, 0])
        fp_in = (xp * (1 + sc_fp) + sh_fp).reshape(-1, D_MODEL)
        fs_in = (xs * (1 + sc_fs) + sh_fs).reshape(-1, D_MODEL)
        f = moe_ffn(jnp.concatenate([fp_in, fs_in], axis=0), moe_w_router[i], moe_b_router[i],
                    moe_w_gu[i], moe_b_gu[i], moe_w_down[i], moe_b_down[i])
        n_p = fp_in.shape[0]
        xp = layer_norm(DEEPNORM_ALPHA * xp + g_fp * f[:n_p].reshape(xp.shape), ln_g[i, 1], ln_b[i, 1])
        xs = layer_norm(DEEPNORM_ALPHA * xs + g_fs * f[n_p:].reshape(xs.shape), ln_g[i, 1], ln_b[i, 1])
    new_cache_k = jnp.stack(ctx_k, axis=1)
    new_cache_v = jnp.stack(ctx_v, axis=1)
    return (xp, xs, new_cache_k, new_cache_v)
```

```python
import functools
import math

import jax
import jax.numpy as jnp
from jax import lax
from jax.experimental import pallas as pl
from jax.experimental.pallas import tpu as pltpu

F32 = jnp.float32
BF16 = jnp.bfloat16
I32 = jnp.int32

LANES = 128
SUBLANES = 8
VMEM_BYTES_V7X = 64 * 1024 * 1024
MIB = 1024 * 1024

GRID_W = 64
N_HEADS = 32
N_KV_HEADS = 8
ROPE_THETA = 10000.0
RMS_EPS = 1e-6
SHORT_CONV = 3
HYENA_EMB_DIM = 33
HYENA_FAST_DECAY = 0.3
HYENA_SLOW_DECAY = 1.5
HYENA_DECAY_TARGET = 1e-2
HYENA_DECAY_SHIFT = 0.05
POOL_WINDOWS = (2, 4, 8, 16)
TOP_K = 4
SWIGLU_LIMIT = 7.0
SWIGLU_ALPHA = 1.702
LN_EPS = 1e-5
N_MOD = 6
COND_ROWS = 8
HIGHEST = lax.Precision.HIGHEST


class Cfg:
    def __init__(self, D, batch, seq, dec_batch, dec_seq, past, depth, n_exp, d_exp,
                 n_heads=N_HEADS, n_kv=N_KV_HEADS, grid_w=GRID_W,
                 mm_tile=512, col_tile=512, moe_tile=256, comb_tile=128, conv_rows=2048,
                 freq_tile=256, filt_tile=256):
        self.D, self.batch, self.seq = D, batch, seq
        self.dec_batch, self.dec_seq, self.past = dec_batch, dec_seq, past
        self.depth, self.E, self.F = depth, n_exp, d_exp
        self.n_heads, self.n_kv, self.grid_w = n_heads, n_kv, grid_w
        self.hd = D // n_heads
        self.gqa = n_heads // n_kv
        self.T_p = batch * seq
        self.T_s = dec_batch * dec_seq
        self.T = self.T_p + self.T_s
        self.RT = seq
        assert dec_seq % self.RT == 0 and self.RT % SUBLANES == 0
        self.NP_T = self.T_p // self.RT
        self.TPS = dec_seq // self.RT
        self.mm_tile = min(mm_tile, self.T_p)
        self.col_tile = min(col_tile, D)
        self.TM = moe_tile
        self.P = self.T * TOP_K + n_exp * self.TM
        self.NT = self.P // self.TM
        self.CT = min(comb_tile, self.RT)
        self.conv_rows = min(conv_rows, self.T_p)
        assert self.T_p % self.conv_rows == 0 and self.conv_rows % seq == 0
        assert dec_seq % self.conv_rows == 0 or self.conv_rows % dec_seq == 0
        self.freq_tile = freq_tile
        self.filt_tile = min(filt_tile, D)
        self.alpha = (2 * depth) ** 0.25
        self.n_pool = len(POOL_WINDOWS)

    def cond_row(self, i):
        return jnp.where(i < self.NP_T, 0, 1 + (i - self.NP_T) // self.TPS)

    def seq_edges(self, i):
        is_p = i < self.NP_T
        k = (i - self.NP_T) % self.TPS
        return is_p | (k == 0), is_p | (k == self.TPS - 1), jnp.where(is_p, 0, k * self.RT)


def _params(sem, vmem_mib):
    return pltpu.CompilerParams(dimension_semantics=sem, vmem_limit_bytes=int(vmem_mib * MIB))


def _silu(x):
    return x * jax.nn.sigmoid(x)


def _mod_kernel(c_ref, w_ref, b_ref, o_ref):
    s = _silu(c_ref[...]).astype(BF16)
    o_ref[...] = jnp.dot(s, w_ref[...].astype(BF16), preferred_element_type=F32) + b_ref[...]


def modulation_all(cfg, cond, mod_w, mod_b):
    depth, D, N = mod_w.shape
    tn = cfg.col_tile
    out = pl.pallas_call(
        _mod_kernel,
        grid=(depth, N // tn),
        in_specs=[pl.BlockSpec((COND_ROWS, D), lambda l, j: (0, 0)),
                  pl.BlockSpec((None, D, tn), lambda l, j: (l, 0, j)),
                  pl.BlockSpec((None, 1, tn), lambda l, j: (l, 0, j))],
        out_specs=pl.BlockSpec((None, COND_ROWS, tn), lambda l, j: (l, 0, j)),
        out_shape=jax.ShapeDtypeStruct((depth, COND_ROWS, N), F32),
        compiler_params=_params(("parallel", "parallel"), 40),
        name="modulation",
    )(cond, mod_w, mod_b.reshape(depth, 1, N))
    return out.reshape(depth, COND_ROWS, N_MOD, D)


def _mod_spec(cfg, layer):
    return pl.BlockSpec((None, None, N_MOD, cfg.D), lambda i, *_: (layer, cfg.cond_row(i), 0, 0))


def _modulate_kernel(x_ref, m_ref, h_ref):
    m = m_ref[...]
    h_ref[...] = (x_ref[...] * (1.0 + m[1:2]) + m[0:1]).astype(h_ref.dtype)


def modulate_first(cfg, x, mod, layer, h_dtype):
    RT, D = cfg.RT, cfg.D
    return pl.pallas_call(
        _modulate_kernel,
        grid=(cfg.T // RT,),
        in_specs=[pl.BlockSpec((RT, D), lambda i: (i, 0)), _mod_spec(cfg, layer)],
        out_specs=pl.BlockSpec((RT, D), lambda i: (i, 0)),
        out_shape=jax.ShapeDtypeStruct((cfg.T, D), h_dtype),
        compiler_params=_params(("parallel",), 32),
        name="modulate_first",
    )(x, mod)


def _deepnorm_ln(x, f, gate, g, b, alpha):
    y = alpha * x + gate * f
    mu = jnp.mean(y, axis=-1, keepdims=True)
    yc = y - mu
    var = jnp.mean(yc * yc, axis=-1, keepdims=True)
    return yc * lax.rsqrt(var + LN_EPS) * g + b


def _ln_router_kernel(x_ref, f_ref, m_ref, g_ref, b_ref, whi_ref, wlo_ref, br_ref,
                      xo_ref, ho_ref, ridx_ref, rprob_ref, cnt_ref, carry_ref,
                      *, n_exp, alpha):
    i = pl.program_id(0)
    RT = x_ref.shape[0]

    @pl.when(i == 0)
    def _():
        carry_ref[...] = jnp.zeros_like(carry_ref)

    m = m_ref[...]
    xn = _deepnorm_ln(x_ref[...], f_ref[...], m[2:3], g_ref[...], b_ref[...], alpha)
    xo_ref[...] = xn
    h = xn * (1.0 + m[4:5]) + m[3:4]
    ho_ref[...] = h
    h_hi = h.astype(BF16)
    h_lo = (h - h_hi.astype(F32)).astype(BF16)
    w_hi = whi_ref[...]
    logits = (jnp.dot(h_hi, w_hi, preferred_element_type=F32)
              + jnp.dot(h_lo, w_hi, preferred_element_type=F32)
              + jnp.dot(h_hi, wlo_ref[...], preferred_element_type=F32)) + br_ref[...]
    lane = lax.broadcasted_iota(I32, (RT, LANES), 1).astype(F32)
    neg = jnp.float32(-jnp.inf)
    l = jnp.where(lane < n_exp, logits, neg)
    vals, idxs = [], []
    multi = jnp.zeros((RT, LANES), F32)
    for _ in range(TOP_K):
        mx = jnp.max(l, axis=-1, keepdims=True)
        ik = jnp.min(jnp.where(l == mx, lane, float(LANES)), axis=-1, keepdims=True)
        sel = lane == ik
        vals.append(mx)
        idxs.append(ik)
        multi = multi + sel.astype(F32)
        l = jnp.where(sel, neg, l)
    exps = [jnp.exp(v - vals[0]) for v in vals]
    den = exps[0]
    for e in exps[1:]:
        den = den + e
    r_i = lax.broadcasted_iota(I32, (RT, RT), 0)
    c_i = lax.broadcasted_iota(I32, (RT, RT), 1)
    tri = (c_i < r_i).astype(BF16)
    prefix = jnp.dot(tri, multi.astype(BF16), preferred_element_type=F32) + carry_ref[0:1, :]
    ridx = jnp.zeros((RT, LANES), F32)
    rprob = jnp.zeros((RT, LANES), F32)
    for k in range(TOP_K):
        rank = jnp.sum(jnp.where(lane == idxs[k], prefix, 0.0), axis=-1, keepdims=True)
        ridx = jnp.where(lane == k, idxs[k], ridx)
        ridx = jnp.where(lane == TOP_K + k, rank, ridx)
        rprob = jnp.where(lane == k, exps[k] / den, rprob)
    ridx_ref[...] = ridx.astype(I32)
    rprob_ref[...] = rprob
    carry = carry_ref[0:1, :] + jnp.sum(multi, axis=0, keepdims=True)
    carry_ref[...] = jnp.broadcast_to(carry, carry_ref.shape)
    cnt_ref[...] = jnp.broadcast_to(carry, cnt_ref.shape).astype(I32)


def ln_router(cfg, x, f, mod, layer, ln_g, ln_b, w_router, b_router):
    RT, D, T, E = cfg.RT, cfg.D, cfg.T, cfg.E
    wr = jnp.zeros((D, LANES), F32).at[:, :E].set(w_router[layer])
    w_hi = wr.astype(BF16)
    w_lo = (wr - w_hi.astype(F32)).astype(BF16)
    br = jnp.zeros((1, LANES), F32).at[0, :E].set(b_router[layer])
    row = lambda i: (i, 0)
    const = lambda i: (0, 0)
    return pl.pallas_call(
        functools.partial(_ln_router_kernel, n_exp=E, alpha=cfg.alpha),
        grid=(T // RT,),
        in_specs=[pl.BlockSpec((RT, D), row), pl.BlockSpec((RT, D), row), _mod_spec(cfg, layer),
                  pl.BlockSpec((None, None, 1, D), lambda i: (layer, 0, 0, 0)),
                  pl.BlockSpec((None, None, 1, D), lambda i: (layer, 0, 0, 0)),
                  pl.BlockSpec((D, LANES), const), pl.BlockSpec((D, LANES), const),
                  pl.BlockSpec((1, LANES), const)],
        out_specs=[pl.BlockSpec((RT, D), row), pl.BlockSpec((RT, D), row),
                   pl.BlockSpec((RT, LANES), row), pl.BlockSpec((RT, LANES), row),
                   pl.BlockSpec((SUBLANES, LANES), const)],
        out_shape=[jax.ShapeDtypeStruct((T, D), F32), jax.ShapeDtypeStruct((T, D), F32),
                   jax.ShapeDtypeStruct((T, LANES), I32), jax.ShapeDtypeStruct((T, LANES), F32),
                   jax.ShapeDtypeStruct((SUBLANES, LANES), I32)],
        scratch_shapes=[pltpu.VMEM((SUBLANES, LANES), F32)],
        compiler_params=_params(("arbitrary",), 48),
        name="ln_router",
    )(x, f, mod, ln_g.reshape(cfg.depth, 2, 1, D), ln_b.reshape(cfg.depth, 2, 1, D), w_hi, w_lo, br)


def _ln_combine_kernel(pos_cur, pos_nxt, x_ref, p_ref, m_ref, g_ref, b_ref, *rest,
                       alpha, n_tiles, has_next):
    if has_next:
        mn_ref, y_hbm, xo_ref, ho_ref, buf, sem = rest
    else:
        y_hbm, xo_ref, buf, sem = rest
    i = pl.program_id(0)
    CT = x_ref.shape[0]
    rows = TOP_K * CT
    slot = i % 2

    def row_copy(src_row, dst_row, s):
        return pltpu.make_async_copy(y_hbm.at[pl.ds(src_row, 1)], buf.at[pl.ds(dst_row, 1)], sem.at[s])

    def issue(pos_ref, s):
        def body(j, c):
            row_copy(pos_ref[j], s * rows + j, s).start()
            return c
        lax.fori_loop(0, rows, body, 0, unroll=8)

    def wait(s):
        def body(j, c):
            row_copy(0, s * rows, s).wait()
            return c
        lax.fori_loop(0, rows, body, 0, unroll=8)

    @pl.when(i == 0)
    def _():
        issue(pos_cur, 0)

    @pl.when(i + 1 < n_tiles)
    def _():
        issue(pos_nxt, 1 - slot)

    wait(slot)
    p = p_ref[...]
    base = pl.multiple_of(slot * rows, rows)
    f = jnp.zeros(x_ref.shape, F32)
    for k in range(TOP_K):
        f = f + p[:, k:k + 1] * buf[pl.ds(base + k * CT, CT), :]
    m = m_ref[...]
    xn = _deepnorm_ln(x_ref[...], f, m[5:6], g_ref[...], b_ref[...], alpha)
    xo_ref[...] = xn
    if has_next:
        mn = mn_ref[...]
        ho_ref[...] = (xn * (1.0 + mn[1:2]) + mn[0:1]).astype(ho_ref.dtype)


def ln_combine(cfg, x, y, pos_flat, rprob, mod, layer, ln_g, ln_b, next_dtype):
    CT, D, T = cfg.CT, cfg.D, cfg.T
    n_tiles = T // CT
    per = cfg.RT // CT
    rows = TOP_K * CT
    has_next = next_dtype is not None
    row = lambda i: (i, 0)
    mod_cur = pl.BlockSpec((None, None, N_MOD, D), lambda i: (layer, cfg.cond_row(i // per), 0, 0))
    in_specs = [pl.BlockSpec((rows,), lambda i: (i,), memory_space=pltpu.SMEM),
                pl.BlockSpec((rows,), lambda i: (jnp.minimum(i + 1, n_tiles - 1),), memory_space=pltpu.SMEM),
                pl.BlockSpec((CT, D), row), pl.BlockSpec((CT, LANES), row), mod_cur,
                pl.BlockSpec((None, None, 1, D), lambda i: (layer, 1, 0, 0)),
                pl.BlockSpec((None, None, 1, D), lambda i: (layer, 1, 0, 0))]
    args = [pos_flat, pos_flat, x, rprob, mod, ln_g.reshape(cfg.depth, 2, 1, D), ln_b.reshape(cfg.depth, 2, 1, D)]
    out_specs = [pl.BlockSpec((CT, D), row)]
    out_shape = [jax.ShapeDtypeStruct((T, D), F32)]
    if has_next:
        in_specs.append(pl.BlockSpec((None, None, N_MOD, D),
                                     lambda i: (layer + 1, cfg.cond_row(i // per), 0, 0)))
        args.append(mod)
        out_specs.append(pl.BlockSpec((CT, D), row))
        out_shape.append(jax.ShapeDtypeStruct((T, D), next_dtype))
    in_specs.append(pl.BlockSpec(memory_space=pl.ANY))
    args.append(y)
    out = pl.pallas_call(
        functools.partial(_ln_combine_kernel, alpha=cfg.alpha, n_tiles=n_tiles, has_next=has_next),
        grid=(n_tiles,),
        in_specs=in_specs, out_specs=out_specs, out_shape=out_shape,
        scratch_shapes=[pltpu.VMEM((2 * rows, D), F32), pltpu.SemaphoreType.DMA((2,))],
        compiler_params=_params(("arbitrary",), 48),
        name="ln_combine",
    )(*args)
    return (out[0], out[1]) if has_next else (out[0], None)


def _mm_kernel(*refs, has_scale, n_first):
    a_refs, refs = (refs[:1], refs[1:]) if n_first is None else (refs[:2], refs[2:])
    if has_scale:
        w_ref, b_ref, s_ref, o_ref, wbf = refs
    else:
        w_ref, b_ref, o_ref, wbf = refs
    i = pl.program_id(2)

    @pl.when(i == 0)
    def _():
        wbf[...] = w_ref[...].astype(BF16)

    def emit(a_ref):
        acc = jnp.dot(a_ref[...], wbf[...], preferred_element_type=F32) + b_ref[...]
        if has_scale:
            acc = acc * s_ref[...]
        o_ref[...] = acc.astype(o_ref.dtype)

    if n_first is None:
        emit(a_refs[0])
    else:
        pl.when(i < n_first)(lambda: emit(a_refs[0]))
        pl.when(i >= n_first)(lambda: emit(a_refs[1]))


def matmul(cfg, a, w4, layer, bias, scale=None, out_dtype=F32, name="matmul"):
    _, G, K, N = w4.shape
    tm, tn = cfg.mm_tile, min(cfg.col_tile, N)
    nj = N // tn
    col = lambda g, j, i: (0, g * nj + j)
    if isinstance(a, tuple):
        n_first = a[0].shape[0] // tm
        M = a[0].shape[0] + a[1].shape[0]
        in_specs = [pl.BlockSpec((tm, K), lambda g, j, i: (jnp.minimum(i, n_first - 1), g)),
                    pl.BlockSpec((tm, K), lambda g, j, i: (jnp.maximum(i - n_first, 0), g))]
        args = list(a)
    else:
        n_first = None
        M = a.shape[0]
        in_specs = [pl.BlockSpec((tm, K), lambda g, j, i: (i, g))]
        args = [a]
    in_specs += [pl.BlockSpec((None, None, K, tn), lambda g, j, i: (layer, g, 0, j)),
                 pl.BlockSpec((1, tn), col)]
    args += [w4, bias]
    if scale is not None:
        in_specs.append(pl.BlockSpec((1, tn), col))
        args.append(scale)
    return pl.pallas_call(
        functools.partial(_mm_kernel, has_scale=scale is not None, n_first=n_first),
        grid=(G, nj, M // tm),
        in_specs=in_specs,
        out_specs=pl.BlockSpec((tm, tn), lambda g, j, i: (i, g * nj + j)),
        out_shape=jax.ShapeDtypeStruct((M, G * N), out_dtype),
        scratch_shapes=[pltpu.VMEM((K, tn), BF16)],
        compiler_params=_params(("parallel", "parallel", "arbitrary"), 48),
        name=name,
    )(*args)


def _hyconv_kernel(*refs, cfg):
    z_refs, w_refs, b_refs = refs[:9], refs[9:12], refs[12:15]
    u_ref, x0_ref = refs[15:17]
    i = pl.program_id(0)
    RT = cfg.RT
    first, last, _ = cfg.seq_edges(i)
    row = lax.broadcasted_iota(I32, u_ref.shape, 0)

    def conv(zc, zp, zn, w, b):
        z = zc[...]
        prev_row = jnp.where(first, 0.0, zp[SUBLANES - 1:SUBLANES, :])
        next_row = jnp.where(last, 0.0, zn[0:1, :])
        z_m = jnp.where(row == 0, prev_row, pltpu.roll(z, 1, 0))
        z_p = jnp.where(row == RT - 1, next_row, pltpu.roll(z, RT - 1, 0))
        wv = w[...]
        return wv[0:1] * z_m + wv[1:2] * z + wv[2:3] * z_p + b[...]

    x0, x1, v = [conv(*z_refs[3 * g:3 * g + 3], w_refs[g], b_refs[g]) for g in range(3)]
    u_ref[...] = v * x1
    x0_ref[...] = x0


def hyena_short_conv(cfg, z, conv_w, conv_b, j):
    RT, D, T, tc = cfg.RT, cfg.D, cfg.T, cfg.col_tile
    nc = D // tc
    r8 = RT // SUBLANES
    in_specs, args = [], []
    for g in range(3):
        in_specs += [pl.BlockSpec((RT, tc), lambda i, c, g=g: (i, g * nc + c)),
                     pl.BlockSpec((SUBLANES, tc), lambda i, c, g=g: (jnp.maximum(i * r8 - 1, 0), g * nc + c)),
                     pl.BlockSpec((SUBLANES, tc),
                                  lambda i, c, g=g: (jnp.minimum((i + 1) * r8, T // SUBLANES - 1), g * nc + c))]
        args += [z, z, z]
    for g in range(3):
        in_specs.append(pl.BlockSpec((None, SHORT_CONV, tc), lambda i, c, g=g: (j, 0, g * nc + c)))
        args.append(conv_w)
    for g in range(3):
        in_specs.append(pl.BlockSpec((None, 1, tc), lambda i, c, g=g: (j, 0, g * nc + c)))
        args.append(conv_b.reshape(conv_b.shape[0], 1, 3 * D))
    blk = pl.BlockSpec((RT, tc), lambda i, c: (i, c))
    return pl.pallas_call(
        functools.partial(_hyconv_kernel, cfg=cfg),
        grid=(T // RT, nc),
        in_specs=in_specs, out_specs=[blk, blk],
        out_shape=[jax.ShapeDtypeStruct((T, D), F32), jax.ShapeDtypeStruct((T, D), F32)],
        compiler_params=_params(("parallel", "parallel"), 32),
        name="hyena_short_conv",
    )(*args)


def _filter_kernel(z_ref, t_ref, sg_ref, w1_ref, w23_ref, b_ref, fr_ref, wf_ref, wb_ref, dl_ref,
                   x1_ref, x2_ref, kl_ref):
    b = b_ref[...]
    fr = fr_ref[...]
    hdn = jnp.sin(fr[0:1] * (jnp.dot(z_ref[...], w1_ref[...], precision=HIGHEST,
                                     preferred_element_type=F32) + b[0:1]))
    for s in range(2):
        hdn = jnp.sin(fr[s + 1:s + 2] * (jnp.dot(hdn, w23_ref[s], precision=HIGHEST,
                                                 preferred_element_type=F32) + b[s + 1:s + 2]))
    t = t_ref[...]
    win = jnp.exp(-t * dl_ref[...]) + HYENA_DECAY_SHIFT
    hf = jnp.dot(hdn, wf_ref[...], precision=HIGHEST, preferred_element_type=F32) * win
    hb = jnp.dot(hdn, wb_ref[...], precision=HIGHEST, preferred_element_type=F32) * win
    row = lax.broadcasted_iota(I32, hb.shape, 0)
    hb = jnp.where(row == 0, 0.0, hb)
    norm = jnp.sum(jnp.abs(hf), axis=0, keepdims=True) + jnp.sum(jnp.abs(hb), axis=0, keepdims=True)
    x1 = (hf + hb) / norm
    x2 = (hb - hf) / norm
    x1_ref[...] = x1.astype(x1_ref.dtype)
    x2_ref[...] = x2.astype(x2_ref.dtype)
    kl_ref[...] = jnp.sum(sg_ref[...] * x1, axis=0, keepdims=True)


def hyena_position_features(L):
    t = jnp.linspace(0.0, 1.0, L, dtype=F32)[:, None]
    bands = (HYENA_EMB_DIM - 1) // 2
    w = 2.0 * math.pi * jnp.arange(L, dtype=F32)[:, None] / L
    f = jnp.linspace(1e-4, bands - 1, bands, dtype=F32)[None, :]
    return t, jnp.concatenate([t, jnp.cos(f * w), -jnp.sin(f * w)], axis=-1)


def hyena_filter_parts(cfg, L, j, w1, w23, b, freq, w_out):
    D, tc = cfg.D, cfg.filt_tile
    width = w1.shape[-1]
    t, z = hyena_position_features(L)
    zp = jnp.zeros((L, LANES), F32).at[:, :HYENA_EMB_DIM].set(z)
    w1p = jnp.zeros((LANES, width), F32).at[:HYENA_EMB_DIM].set(w1[j])
    sign = (1 - 2 * (jnp.arange(L) % 2)).astype(F32)[:, None]
    min_decay = math.log(HYENA_DECAY_TARGET) / HYENA_SLOW_DECAY
    max_decay = math.log(HYENA_DECAY_TARGET) / HYENA_FAST_DECAY
    deltas = jnp.abs(jnp.linspace(min_decay, max_decay, D, dtype=F32))[None, :]
    nc = D // tc
    const2 = lambda c: (0, 0)
    return pl.pallas_call(
        _filter_kernel,
        grid=(nc,),
        in_specs=[pl.BlockSpec((L, LANES), const2), pl.BlockSpec((L, 1), const2), pl.BlockSpec((L, 1), const2),
                  pl.BlockSpec((LANES, width), const2),
                  pl.BlockSpec((None, 2, width, width), lambda c: (j, 0, 0, 0)),
                  pl.BlockSpec((None, 3, width), lambda c: (j, 0, 0)),
                  pl.BlockSpec((None, 3, width), lambda c: (j, 0, 0)),
                  pl.BlockSpec((None, width, tc), lambda c: (j, 0, c)),
                  pl.BlockSpec((None, width, tc), lambda c: (j, 0, nc + c)),
                  pl.BlockSpec((1, tc), lambda c: (0, c))],
        out_specs=[pl.BlockSpec((L, tc), lambda c: (0, c)), pl.BlockSpec((L, tc), lambda c: (0, c)),
                   pl.BlockSpec((1, tc), lambda c: (0, c))],
        out_shape=[jax.ShapeDtypeStruct((L, D), BF16), jax.ShapeDtypeStruct((L, D), BF16),
                   jax.ShapeDtypeStruct((1, D), F32)],
        compiler_params=_params(("parallel",), 48),
        name="hyena_filter",
    )(zp, t, sign, w1p, w23, b, freq, w_out, w_out, deltas)


def dft_matrices(L):
    f = jnp.arange(L, dtype=I32)[:, None]
    t = jnp.arange(L, dtype=I32)[None, :]
    ang = ((f * t) % (2 * L)).astype(F32) * (math.pi / L)
    c, s = jnp.cos(ang), jnp.sin(ang)
    sign = (1 - 2 * (t % 2)).astype(F32)
    fwd = jnp.concatenate([c, jnp.where(f == 0, sign, s)], axis=0)
    ga = c * jnp.where(f == 0, 0.5, 1.0) / L
    gb = jnp.where(f == 0, sign / (2 * L), -s / L)
    inv = jnp.concatenate([ga.T, gb.T], axis=1)
    return fwd.astype(BF16), inv.astype(BF16)


def _kf_kernel(f_ref, x1_ref, x2_ref, kl_ref, o_ref, *, nk):
    k2 = pl.program_id(1)

    @pl.when(k2 < nk)
    def _():
        o_ref[...] = jnp.dot(f_ref[...], x1_ref[...], preferred_element_type=F32)

    @pl.when(k2 >= nk)
    def _():
        o_ref[...] = jnp.dot(f_ref[...], x2_ref[...], preferred_element_type=F32)

    @pl.when(k2 == nk)
    def _():
        o_ref[0:1, :] = kl_ref[...]


def hyena_filter_spectrum(cfg, L, fwd, x1, x2, kl):
    D, tc = cfg.D, cfg.col_tile
    tf = min(cfg.freq_tile, L)
    nk = L // tf
    return pl.pallas_call(
        functools.partial(_kf_kernel, nk=nk),
        grid=(D // tc, 2 * nk),
        in_specs=[pl.BlockSpec((tf, L), lambda c, k: (k, 0)),
                  pl.BlockSpec((L, tc), lambda c, k: (0, c)), pl.BlockSpec((L, tc), lambda c, k: (0, c)),
                  pl.BlockSpec((1, tc), lambda c, k: (0, c))],
        out_specs=pl.BlockSpec((tf, tc), lambda c, k: (k, c)),
        out_shape=jax.ShapeDtypeStruct((2 * L, D), F32),
        compiler_params=_params(("parallel", "arbitrary"), 32),
        name="hyena_filter_spectrum",
    )(fwd, x1, x2, kl)


def _lconv_kernel(u_ref, x0_ref, fb_ref, fa_ref, fs_ref, ga_ref, gs_ref, ka_ref, ks_ref, o_ref, acc,
                  *, L, n_seq, nk):
    k = pl.program_id(2)

    @pl.when(k == 0)
    def _():
        acc[...] = jnp.zeros_like(acc)

    ka, ks = ka_ref[...], ks_ref[...]
    row0 = (lax.broadcasted_iota(I32, ka.shape, 0) == 0) & (k == 0)
    for s in range(n_seq):
        rows = pl.ds(s * L, L)
        us = u_ref[rows, :].astype(BF16)
        a = jnp.dot(fa_ref[...], us, preferred_element_type=F32)
        b = jnp.dot(fs_ref[...], us, preferred_element_type=F32)
        bk = b * ks
        ya = a * ka + jnp.where(row0, 0.0, bk)
        ys = jnp.where(row0, bk, a * ks - b * ka)
        acc[rows, :] += (jnp.dot(ga_ref[...], ya.astype(BF16), preferred_element_type=F32)
                         + jnp.dot(gs_ref[...], ys.astype(BF16), preferred_element_type=F32))

    @pl.when(k == nk - 1)
    def _():
        o_ref[...] = ((acc[...] + u_ref[...] * fb_ref[...]) * x0_ref[...]).astype(o_ref.dtype)


def hyena_long_conv(cfg, L, row_off, n_rows, u, x0, filter_bias, j, fwd, inv, kf):
    D, tc = cfg.D, cfg.col_tile
    RB = max(L, cfg.conv_rows)
    n_seq = RB // L
    tf = min(cfg.freq_tile, L)
    nk = L // tf
    off = row_off // RB
    blk = pl.BlockSpec((RB, tc), lambda r, c, k: (off + r, c))
    in_specs = [blk, blk,
                pl.BlockSpec((None, 1, tc), lambda r, c, k: (j, 0, c)),
                pl.BlockSpec((tf, L), lambda r, c, k: (k, 0)), pl.BlockSpec((tf, L), lambda r, c, k: (nk + k, 0)),
                pl.BlockSpec((L, tf), lambda r, c, k: (0, k)), pl.BlockSpec((L, tf), lambda r, c, k: (0, nk + k)),
                pl.BlockSpec((tf, tc), lambda r, c, k: (k, c)), pl.BlockSpec((tf, tc), lambda r, c, k: (nk + k, c))]
    args = [u, x0, filter_bias.reshape(filter_bias.shape[0], 1, D), fwd, fwd, inv, inv, kf, kf]
    return pl.pallas_call(
        functools.partial(_lconv_kernel, L=L, n_seq=n_seq, nk=nk),
        grid=(n_rows // RB, D // tc, nk),
        in_specs=in_specs, out_specs=pl.BlockSpec((RB, tc), lambda r, c, k: (r, c)),
        out_shape=jax.ShapeDtypeStruct((n_rows, D), BF16),
        scratch_shapes=[pltpu.VMEM((RB, tc), F32)],
        compiler_params=_params(("parallel", "parallel", "arbitrary"), 48),
        name="hyena_long_conv",
    )(*args)


def hyena_mixer(cfg, h, j, hy_w_in, hy_b_in, hy_conv_w, hy_conv_b, filt, hy_filter_bias, hy_w_out, hy_b_out):
    D = cfg.D
    z = matmul(cfg, h, hy_w_in.reshape(hy_w_in.shape[0], 1, D, 3 * D), j, hy_b_in[j][None, :], name="hyena_in")
    u, x0 = hyena_short_conv(cfg, z, hy_conv_w, hy_conv_b, j)
    out = []
    for L, row_off, n_rows in ((cfg.seq, 0, cfg.T_p), (cfg.dec_seq, cfg.T_p, cfg.T_s)):
        x1, x2, kl = hyena_filter_parts(cfg, L, j, *filt)
        fwd, inv = dft_matrices(L)
        kf = hyena_filter_spectrum(cfg, L, fwd, x1, x2, kl)
        out.append(hyena_long_conv(cfg, L, row_off, n_rows, u, x0, hy_filter_bias, j, fwd, inv, kf))
    return matmul(cfg, tuple(out), hy_w_out.reshape(hy_w_out.shape[0], 1, D, D), j, hy_b_out[j][None, :], name="hyena_out")


def _qk_post_kernel(x_ref, g_ref, cos_ref, sin_ref, o_ref, *rest, cfg, emit_norm):
    i = pl.program_id(0)
    hd = cfg.hd
    g = g_ref[...]
    lane = lax.broadcasted_iota(I32, (x_ref.shape[0], hd), 1)
    low = (lane % (hd // 2)) < (hd // 4)
    is_latent = i >= cfg.NP_T
    for hh in range(x_ref.shape[1] // hd):
        cols = slice(hh * hd, (hh + 1) * hd)
        x = x_ref[:, cols]
        xn = x * lax.rsqrt(jnp.mean(x * x, axis=-1, keepdims=True) + RMS_EPS) * g
        if emit_norm:
            rest[0][:, cols] = xn
        swapped = jnp.where(low, pltpu.roll(xn, hd - hd // 4, 1), pltpu.roll(xn, hd // 4, 1))
        roped = xn * cos_ref[...] + swapped * sin_ref[...]
        o_ref[:, cols] = jnp.where(is_latent, roped, xn).astype(o_ref.dtype)


def axial_rope_tables(cfg):
    L, hd = cfg.dec_seq, cfg.hd
    axis_dim = hd // 2
    rows = L // cfg.grid_w
    row = jnp.repeat(jnp.arange(rows), cfg.grid_w)
    col = jnp.tile(jnp.arange(cfg.grid_w), rows)
    inv_freq = ROPE_THETA ** (-jnp.arange(0, axis_dim, 2, dtype=F32) / axis_dim)
    ang = jnp.stack([row, col], axis=-1).astype(F32)[..., None] * inv_freq
    cos, sin = jnp.cos(ang), jnp.sin(ang)
    cos_t = jnp.concatenate([cos, cos], axis=-1).reshape(L, hd)
    sin_t = jnp.concatenate([-sin, sin], axis=-1).reshape(L, hd)
    return cos_t, sin_t


def qk_post(cfg, qkv, norm_w, j, cos_t, sin_t, col_off, n_cols, emit_norm):
    RT, T, hd = cfg.RT, cfg.T, cfg.hd
    tc = min(cfg.col_tile, n_cols)
    blk_off = col_off // tc
    pos_blk = lambda i, c: (jnp.where(i < cfg.NP_T, 0, (i - cfg.NP_T) % cfg.TPS), 0)
    out_specs = [pl.BlockSpec((RT, tc), lambda i, c: (i, c))]
    out_shape = [jax.ShapeDtypeStruct((T, n_cols), BF16)]
    if emit_norm:
        out_specs.append(pl.BlockSpec((RT, tc), lambda i, c: (i, c)))
        out_shape.append(jax.ShapeDtypeStruct((T, n_cols), F32))
    return pl.pallas_call(
        functools.partial(_qk_post_kernel, cfg=cfg, emit_norm=emit_norm),
        grid=(T // RT, n_cols // tc),
        in_specs=[pl.BlockSpec((RT, tc), lambda i, c: (i, blk_off + c)),
                  pl.BlockSpec((None, 1, hd), lambda i, c: (j, 0, 0)),
                  pl.BlockSpec((RT, hd), pos_blk), pl.BlockSpec((RT, hd), pos_blk)],
        out_specs=out_specs, out_shape=out_shape,
        compiler_params=_params(("parallel", "parallel"), 32),
        name="qk_norm_rope",
    )(qkv, norm_w.reshape(norm_w.shape[0], 1, hd), cos_t, sin_t)


def _attn_kernel(*refs, cfg, has_cache):
    if has_cache:
        q_ref, k_ref, v_ref, kc_ref, vc_ref, o_ref = refs
    else:
        q_ref, k_ref, v_ref, o_ref = refs
    hd = cfg.hd
    scale = hd ** -0.5
    nt = (((1,), (1,)), ((), ()))
    k = k_ref[...]
    v = v_ref[...].astype(BF16)
    if has_cache:
        kc = kc_ref[...].astype(BF16)
        vc = vc_ref[...].astype(BF16)
    for g in range(cfg.gqa):
        cols = slice(g * hd, (g + 1) * hd)
        q = q_ref[:, cols]
        s = lax.dot_general(q, k, nt, preferred_element_type=F32) * scale
        m = jnp.max(s, axis=-1, keepdims=True)
        if has_cache:
            sc = lax.dot_general(q, kc, nt, preferred_element_type=F32) * scale
            m = jnp.maximum(m, jnp.max(sc, axis=-1, keepdims=True))
        p = jnp.exp(s - m)
        l = jnp.sum(p, axis=-1, keepdims=True)
        o = jnp.dot(p.astype(BF16), v, preferred_element_type=F32)
        if has_cache:
            pc = jnp.exp(sc - m)
            l = l + jnp.sum(pc, axis=-1, keepdims=True)
            o = o + jnp.dot(pc.astype(BF16), vc, preferred_element_type=F32)
        o_ref[:, cols] = (o / l).astype(o_ref.dtype)


def attention(cfg, q, k, qkv, n_seq, L, row_off, cache):
    RT, D, hd, G = cfg.RT, cfg.D, cfg.hd, cfg.gqa
    v_col = cfg.n_heads + cfg.n_kv
    q_off, kv_off, nq = row_off // RT, row_off // L, L // RT
    in_specs = [pl.BlockSpec((RT, G * hd), lambda b, h, t: (q_off + b * nq + t, h)),
                pl.BlockSpec((L, hd), lambda b, h, t: (kv_off + b, h)),
                pl.BlockSpec((L, hd), lambda b, h, t: (kv_off + b, v_col + h))]
    args = [q, k, qkv]
    if cache is not None:
        ck, cv, n_attn, j = cache
        past = ck.shape[0] // (n_seq * n_attn)
        spec = pl.BlockSpec((past, hd), lambda b, h, t: (b * n_attn + j, h))
        in_specs += [spec, spec]
        args += [ck, cv]
    return pl.pallas_call(
        functools.partial(_attn_kernel, cfg=cfg, has_cache=cache is not None),
        grid=(n_seq, cfg.n_kv, nq),
        in_specs=in_specs,
        out_specs=pl.BlockSpec((RT, G * hd), lambda b, h, t: (b * nq + t, h)),
        out_shape=jax.ShapeDtypeStruct((n_seq * L, D), BF16),
        compiler_params=_params(("parallel", "parallel", "parallel"), 48),
        name="attention",
    )(*args)


def attention_mixer(cfg, h, j, cache_k, cache_v, at_w_qkv, at_q_norm, at_k_norm, at_w_o):
    D, hd = cfg.D, cfg.hd
    n_attn = at_w_qkv.shape[0]
    qkv_dim = at_w_qkv.shape[-1]
    kv_dim = cfg.n_kv * hd
    qkv = matmul(cfg, h, at_w_qkv.reshape(n_attn, 1, D, qkv_dim), j, jnp.zeros((1, qkv_dim), F32), name="attn_qkv")
    cos_t, sin_t = axial_rope_tables(cfg)
    (q,) = qk_post(cfg, qkv, at_q_norm, j, cos_t, sin_t, 0, D, False)
    k, k_norm = qk_post(cfg, qkv, at_k_norm, j, cos_t, sin_t, D, kv_dim, True)
    o_p = attention(cfg, q, k, qkv, cfg.batch, cfg.seq, 0, None)
    ck = cache_k.reshape(-1, kv_dim)
    cv = cache_v.reshape(-1, kv_dim)
    o_s = attention(cfg, q, k, qkv, cfg.dec_batch, cfg.dec_seq, cfg.T_p, (ck, cv, n_attn, j))
    f = matmul(cfg, (o_p, o_s), at_w_o.reshape(n_attn, 1, D, D), j, jnp.zeros((1, D), F32), name="attn_out")
    new_k = k_norm[:cfg.T_p].reshape(cfg.batch, cfg.seq, cfg.n_kv, hd)
    new_v = qkv[:cfg.T_p, D + kv_dim:].reshape(cfg.batch, cfg.seq, cfg.n_kv, hd)
    return f, new_k, new_v


def _pool_kernel(hc, hp, hn, o_ref, ext, *, cfg, cols_per_group):
    i = pl.program_id(0)
    c = pl.program_id(1)
    RT = cfg.RT
    H = SUBLANES
    first, last, p0 = cfg.seq_edges(i)
    L = jnp.where(i < cfg.NP_T, cfg.seq, cfg.dec_seq)
    ext[0:H, :] = jnp.where(first, 0.0, hp[...])
    ext[H:H + RT, :] = hc[...]
    ext[H + RT:, :] = jnp.where(last, 0.0, hn[...])
    pos = p0 + lax.broadcasted_iota(I32, (RT, 1), 0)
    for gi, w in enumerate(POOL_WINDOWS):
        @pl.when(c // cols_per_group == gi)
        def _(w=w):
            half = w // 2
            s = ext[H - half:H - half + RT, :]
            for o in range(1 - half, half):
                s = s + ext[H + o:H + o + RT, :]
            lo = jnp.clip(pos - half, 0, L)
            hi = jnp.clip(pos - half + w, 0, L)
            mean = s / (hi - lo).astype(F32)
            o_ref[...] = (mean - hc[...]).astype(o_ref.dtype)


def pool_mixer(cfg, h, j, pl_w, pl_b, pl_scale):
    RT, D, T = cfg.RT, cfg.D, cfg.T
    gd = D // cfg.n_pool
    tc = min(cfg.col_tile, gd)
    r8 = RT // SUBLANES
    assert max(POOL_WINDOWS) // 2 <= SUBLANES
    d = pl.pallas_call(
        functools.partial(_pool_kernel, cfg=cfg, cols_per_group=gd // tc),
        grid=(T // RT, D // tc),
        in_specs=[pl.BlockSpec((RT, tc), lambda i, c: (i, c)),
                  pl.BlockSpec((SUBLANES, tc), lambda i, c: (jnp.maximum(i * r8 - 1, 0), c)),
                  pl.BlockSpec((SUBLANES, tc), lambda i, c: (jnp.minimum((i + 1) * r8, T // SUBLANES - 1), c))],
        out_specs=pl.BlockSpec((RT, tc), lambda i, c: (i, c)),
        out_shape=jax.ShapeDtypeStruct((T, D), BF16),
        scratch_shapes=[pltpu.VMEM((RT + 2 * SUBLANES, tc), F32)],
        compiler_params=_params(("parallel", "parallel"), 32),
        name="pool_window",
    )(h, h, h)
    return matmul(cfg, d, pl_w, j, pl_b[j][None, :], scale=pl_scale[j][None, :], name="pool_proj")


def _cast_kernel(x_ref, o_ref):
    o_ref[...] = x_ref[...].astype(o_ref.dtype)


def cast_bf16(w, rows=1024):
    shape = w.shape
    w2 = w.reshape(-1, shape[-1])
    rows = min(rows, w2.shape[0])
    out = pl.pallas_call(
        _cast_kernel,
        grid=(w2.shape[0] // rows,),
        in_specs=[pl.BlockSpec((rows, shape[-1]), lambda i: (i, 0))],
        out_specs=pl.BlockSpec((rows, shape[-1]), lambda i: (i, 0)),
        out_shape=jax.ShapeDtypeStruct(w2.shape, BF16),
        compiler_params=_params(("parallel",), 48),
        name="cast_bf16",
    )(w2)
    return out.reshape(shape)


def _moe_up_kernel(te_ref, nu_ref, src_cur, src_nxt, h_hbm, w_ref, b_ref, o_ref, buf, sem, *, F):
    t = pl.program_id(0)
    n_used = nu_ref[0]
    TM = o_ref.shape[0]
    slot = t % 2

    def row_copy(src_row, dst_row, s):
        return pltpu.make_async_copy(h_hbm.at[pl.ds(src_row, 1)], buf.at[pl.ds(dst_row, 1)], sem.at[s])

    def issue(src_ref, s):
        def body(r, c):
            row_copy(src_ref[r], s * TM + r, s).start()
            return c
        lax.fori_loop(0, TM, body, 0, unroll=8)

    def wait(s):
        def body(r, c):
            row_copy(0, s * TM, s).wait()
            return c
        lax.fori_loop(0, TM, body, 0, unroll=8)

    @pl.when((t == 0) & (n_used > 0))
    def _():
        issue(src_cur, 0)

    @pl.when(t + 1 < n_used)
    def _():
        issue(src_nxt, 1 - slot)

    @pl.when(t < n_used)
    def _():
        wait(slot)
        x = buf[pl.ds(pl.multiple_of(slot * TM, TM), TM), :].astype(BF16)
        gu = jnp.dot(x, w_ref[...], preferred_element_type=F32) + b_ref[...]
        g = jnp.minimum(gu[:, :F], SWIGLU_LIMIT)
        u = jnp.clip(gu[:, F:], -SWIGLU_LIMIT, SWIGLU_LIMIT)
        o_ref[...] = (g * jax.nn.sigmoid(SWIGLU_ALPHA * g) * (u + 1.0)).astype(o_ref.dtype)

    @pl.when(t >= n_used)
    def _():
        o_ref[...] = jnp.zeros_like(o_ref)


def moe_up(cfg, h, src, tile_expert, n_used, w_gu_bf, b_gu, layer):
    TM, D, F, NT = cfg.TM, cfg.D, cfg.F, cfg.NT
    grid_spec = pltpu.PrefetchScalarGridSpec(
        num_scalar_prefetch=2,
        grid=(NT,),
        in_specs=[pl.BlockSpec((TM,), lambda t, te, nu: (t,), memory_space=pltpu.SMEM),
                  pl.BlockSpec((TM,), lambda t, te, nu: (jnp.minimum(t + 1, NT - 1),), memory_space=pltpu.SMEM),
                  pl.BlockSpec(memory_space=pl.ANY),
                  pl.BlockSpec((None, None, D, 2 * F), lambda t, te, nu: (layer, te[t], 0, 0)),
                  pl.BlockSpec((None, None, 1, 2 * F), lambda t, te, nu: (layer, te[t], 0, 0))],
        out_specs=pl.BlockSpec((TM, F), lambda t, te, nu: (t, 0)),
        scratch_shapes=[pltpu.VMEM((2 * TM, D), F32), pltpu.SemaphoreType.DMA((2,))])
    return pl.pallas_call(
        functools.partial(_moe_up_kernel, F=F),
        grid_spec=grid_spec,
        out_shape=jax.ShapeDtypeStruct((cfg.P, F), BF16),
        compiler_params=_params(("arbitrary",), 56),
        name="moe_up",
    )(tile_expert, n_used, src, src, h, w_gu_bf, b_gu.reshape(cfg.depth, cfg.E, 1, 2 * F))


def _moe_down_kernel(te_ref, nu_ref, a_ref, w_ref, b_ref, o_ref):
    t = pl.program_id(0)

    @pl.when(t < nu_ref[0])
    def _():
        o_ref[...] = jnp.dot(a_ref[...], w_ref[...], preferred_element_type=F32) + b_ref[...]

    @pl.when(t >= nu_ref[0])
    def _():
        o_ref[...] = jnp.zeros_like(o_ref)


def moe_down(cfg, a, tile_expert, n_used, w_down_bf, b_down, layer):
    TM, D, F, NT = cfg.TM, cfg.D, cfg.F, cfg.NT
    grid_spec = pltpu.PrefetchScalarGridSpec(
        num_scalar_prefetch=2,
        grid=(NT,),
        in_specs=[pl.BlockSpec((TM, F), lambda t, te, nu: (t, 0)),
                  pl.BlockSpec((None, None, F, D), lambda t, te, nu: (layer, te[t], 0, 0)),
                  pl.BlockSpec((None, None, 1, D), lambda t, te, nu: (layer, te[t], 0, 0))],
        out_specs=pl.BlockSpec((TM, D), lambda t, te, nu: (t, 0)))
    return pl.pallas_call(
        _moe_down_kernel,
        grid_spec=grid_spec,
        out_shape=jax.ShapeDtypeStruct((cfg.P, D), F32),
        compiler_params=_params(("arbitrary",), 48),
        name="moe_down",
    )(tile_expert, n_used, a, w_down_bf, b_down.reshape(cfg.depth, cfg.E, 1, D))


def route_metadata(cfg, ridx, cnt):
    T, E, TM, NT, CT = cfg.T, cfg.E, cfg.TM, cfg.NT, cfg.CT
    idx = ridx[:, :TOP_K]
    rank = ridx[:, TOP_K:2 * TOP_K]
    counts = cnt[0, :E]
    padded = ((counts + TM - 1) // TM) * TM
    ends = jnp.cumsum(padded)
    starts = ends - padded
    pos = starts[idx] + rank
    token = jnp.broadcast_to(jnp.arange(T, dtype=I32)[:, None], (T, TOP_K))
    src = jnp.zeros((cfg.P,), I32).at[pos.reshape(-1)].set(token.reshape(-1), unique_indices=True)
    n_used = (ends[-1] // TM).astype(I32)
    tile = jnp.arange(NT, dtype=I32)
    te = jnp.searchsorted(ends, jnp.minimum(tile, n_used - 1) * TM, side="right").astype(I32)
    te = jnp.minimum(te, E - 1)
    pos_flat = pos.reshape(T // CT, CT, TOP_K).transpose(0, 2, 1).reshape(-1).astype(I32)
    return src, te, n_used.reshape(1), pos_flat


def kernel(x_prompt, x_sample, cache_k, cache_v, c, c_ctx, mod_w, mod_b, ln_g, ln_b, hy_w_in, hy_b_in, hy_conv_w, hy_conv_b, hy_ffn_w1, hy_ffn_w23, hy_ffn_b, hy_sin_freq, hy_ffn_w_out, hy_filter_bias, hy_w_out, hy_b_out, at_w_qkv, at_q_norm, at_k_norm, at_w_o, pl_w, pl_b, pl_scale, moe_w_router, moe_b_router, moe_w_gu, moe_b_gu, moe_w_down, moe_b_down):
    batch, seq, D = x_prompt.shape
    dec_batch, dec_seq, _ = x_sample.shape
    depth = mod_w.shape[0]
    cfg = Cfg(D, batch, seq, dec_batch, dec_seq, cache_k.shape[2], depth,
              moe_w_router.shape[-1], moe_w_down.shape[2])
    return trunk_step(cfg, x_prompt, x_sample, cache_k, cache_v, c, c_ctx, mod_w, mod_b, ln_g, ln_b,
                      hy_w_in, hy_b_in, hy_conv_w, hy_conv_b, hy_ffn_w1, hy_ffn_w23, hy_ffn_b, hy_sin_freq,
                      hy_ffn_w_out, hy_filter_bias, hy_w_out, hy_b_out, at_w_qkv, at_q_norm, at_k_norm, at_w_o,
                      pl_w, pl_b, pl_scale, moe_w_router, moe_b_router, moe_w_gu, moe_b_gu, moe_w_down,
                      moe_b_down)


def trunk_step(cfg, x_prompt, x_sample, cache_k, cache_v, c, c_ctx, mod_w, mod_b, ln_g, ln_b,
               hy_w_in, hy_b_in, hy_conv_w, hy_conv_b, hy_ffn_w1, hy_ffn_w23, hy_ffn_b, hy_sin_freq,
               hy_ffn_w_out, hy_filter_bias, hy_w_out, hy_b_out, at_w_qkv, at_q_norm, at_k_norm, at_w_o,
               pl_w, pl_b, pl_scale, moe_w_router, moe_b_router, moe_w_gu, moe_b_gu, moe_w_down, moe_b_down):
    D, depth = cfg.D, cfg.depth
    n_mixers = 3
    assert cfg.dec_batch + 1 <= COND_ROWS
    cond = jnp.zeros((COND_ROWS, D), F32).at[0].set(c_ctx).at[1:1 + cfg.dec_batch].set(c)
    mod = modulation_all(cfg, cond, mod_w, mod_b)
    w_gu_bf = cast_bf16(moe_w_gu)
    w_down_bf = cast_bf16(moe_w_down)

    x = jnp.concatenate([x_prompt.reshape(cfg.T_p, D), x_sample.reshape(cfg.T_s, D)], axis=0)
    mixer_in_dtype = lambda i: F32 if i % n_mixers == 2 else BF16
    h = modulate_first(cfg, x, mod, 0, mixer_in_dtype(0))
    ctx_k, ctx_v = [], []
    for i in range(depth):
        kind, j = i % n_mixers, i // n_mixers
        if kind == 0:
            filt = (hy_ffn_w1, hy_ffn_w23, hy_ffn_b, hy_sin_freq, hy_ffn_w_out)
            f = hyena_mixer(cfg, h, j, hy_w_in, hy_b_in, hy_conv_w, hy_conv_b, filt,
                            hy_filter_bias, hy_w_out, hy_b_out)
        elif kind == 1:
            f, new_k, new_v = attention_mixer(cfg, h, j, cache_k, cache_v, at_w_qkv, at_q_norm, at_k_norm, at_w_o)
            ctx_k.append(new_k)
            ctx_v.append(new_v)
        else:
            f = pool_mixer(cfg, h, j, pl_w, pl_b, pl_scale)
        x, h_ffn, ridx, rprob, cnt = ln_router(cfg, x, f, mod, i, ln_g, ln_b, moe_w_router, moe_b_router)
        src, tile_expert, n_used, pos_flat = route_metadata(cfg, ridx, cnt)
        a = moe_up(cfg, h_ffn, src, tile_expert, n_used, w_gu_bf, moe_b_gu, i)
        y = moe_down(cfg, a, tile_expert, n_used, w_down_bf, moe_b_down, i)
        next_dtype = mixer_in_dtype(i + 1) if i + 1 < depth else None
        x, h = ln_combine(cfg, x, y, pos_flat, rprob, mod, i, ln_g, ln_b, next_dtype)
    xp = x[:cfg.T_p].reshape(cfg.batch, cfg.seq, D)
    xs = x[cfg.T_p:].reshape(cfg.dec_batch, cfg.dec_seq, D)
    return (xp, xs, jnp.stack(ctx_k, axis=1), jnp.stack(ctx_v, axis=1))
```

```python
import functools
import math

import jax
import jax.numpy as jnp
from jax import lax
from jax.experimental import pallas as pl
from jax.experimental.pallas import tpu as pltpu

F32 = jnp.float32
BF16 = jnp.bfloat16
I32 = jnp.int32

LANES = 128
SUBLANES = 8
VMEM_BYTES_V7X = 64 * 1024 * 1024
MIB = 1024 * 1024
MM_VMEM_LIMIT_MIB = 48

GRID_W = 64
N_HEADS = 32
N_KV_HEADS = 8
ROPE_THETA = 10000.0
RMS_EPS = 1e-6
SHORT_CONV = 3
HYENA_EMB_DIM = 33
HYENA_FAST_DECAY = 0.3
HYENA_SLOW_DECAY = 1.5
HYENA_DECAY_TARGET = 1e-2
HYENA_DECAY_SHIFT = 0.05
POOL_WINDOWS = (2, 4, 8, 16)
TOP_K = 4
SWIGLU_LIMIT = 7.0
SWIGLU_ALPHA = 1.702
LN_EPS = 1e-5
N_MOD = 6
COND_ROWS = 8
HIGHEST = lax.Precision.HIGHEST


class Cfg:
    def __init__(self, D, batch, seq, dec_batch, dec_seq, past, depth, n_exp, d_exp,
                 n_heads=N_HEADS, n_kv=N_KV_HEADS, grid_w=GRID_W,
                 mm_tile=1024, col_tile=512, moe_tile=256, comb_tile=128, conv_rows=2048,
                 freq_tile=256, filt_tile=256):
        self.D, self.batch, self.seq = D, batch, seq
        self.dec_batch, self.dec_seq, self.past = dec_batch, dec_seq, past
        self.depth, self.E, self.F = depth, n_exp, d_exp
        self.n_heads, self.n_kv, self.grid_w = n_heads, n_kv, grid_w
        self.hd = D // n_heads
        self.gqa = n_heads // n_kv
        self.T_p = batch * seq
        self.T_s = dec_batch * dec_seq
        self.T = self.T_p + self.T_s
        self.RT = seq
        assert dec_seq % self.RT == 0 and self.RT % SUBLANES == 0
        self.NP_T = self.T_p // self.RT
        self.TPS = dec_seq // self.RT
        self.mm_tile = min(mm_tile, self.T_p)
        self.col_tile = min(col_tile, D)
        self.TM = moe_tile
        self.P = self.T * TOP_K + n_exp * self.TM
        self.NT = self.P // self.TM
        self.CT = min(comb_tile, self.RT)
        self.conv_rows = min(conv_rows, self.T_p)
        assert self.T_p % self.conv_rows == 0 and self.conv_rows % seq == 0
        assert dec_seq % self.conv_rows == 0 or self.conv_rows % dec_seq == 0
        self.freq_tile = freq_tile
        self.filt_tile = min(filt_tile, D)
        self.alpha = (2 * depth) ** 0.25
        self.n_pool = len(POOL_WINDOWS)

    def cond_row(self, i):
        return jnp.where(i < self.NP_T, 0, 1 + (i - self.NP_T) // self.TPS)

    def seq_edges(self, i):
        is_p = i < self.NP_T
        k = (i - self.NP_T) % self.TPS
        return is_p | (k == 0), is_p | (k == self.TPS - 1), jnp.where(is_p, 0, k * self.RT)


def _params(sem, vmem_mib):
    return pltpu.CompilerParams(dimension_semantics=sem, vmem_limit_bytes=int(vmem_mib * MIB))


def _silu(x):
    return x * jax.nn.sigmoid(x)


def _mod_kernel(c_ref, w_ref, b_ref, o_ref):
    s = _silu(c_ref[...]).astype(BF16)
    o_ref[...] = jnp.dot(s, w_ref[...].astype(BF16), preferred_element_type=F32) + b_ref[...]


def modulation_all(cfg, cond, mod_w, mod_b):
    depth, D, N = mod_w.shape
    tn = cfg.col_tile
    out = pl.pallas_call(
        _mod_kernel,
        grid=(depth, N // tn),
        in_specs=[pl.BlockSpec((COND_ROWS, D), lambda l, j: (0, 0)),
                  pl.BlockSpec((None, D, tn), lambda l, j: (l, 0, j)),
                  pl.BlockSpec((None, 1, tn), lambda l, j: (l, 0, j))],
        out_specs=pl.BlockSpec((None, COND_ROWS, tn), lambda l, j: (l, 0, j)),
        out_shape=jax.ShapeDtypeStruct((depth, COND_ROWS, N), F32),
        compiler_params=_params(("parallel", "parallel"), 40),
        name="modulation",
    )(cond, mod_w, mod_b.reshape(depth, 1, N))
    return out.reshape(depth, COND_ROWS, N_MOD, D)


def _mod_spec(cfg, layer):
    return pl.BlockSpec((None, None, N_MOD, cfg.D), lambda i, *_: (layer, cfg.cond_row(i), 0, 0))


def _split_rows(cfg, x):
    RT, D = cfg.RT, cfg.D
    if isinstance(x, tuple):
        return ([pl.BlockSpec((RT, D), lambda i: (jnp.minimum(i, cfg.NP_T - 1), 0)),
                 pl.BlockSpec((RT, D), lambda i: (jnp.maximum(i - cfg.NP_T, 0), 0))], list(x))
    return [pl.BlockSpec((RT, D), lambda i: (i, 0))], [x]


def _load_rows(cfg, x_refs):
    if len(x_refs) == 1:
        return x_refs[0][...]
    return jnp.where(pl.program_id(0) < cfg.NP_T, x_refs[0][...], x_refs[1][...])


def _modulate_kernel(*refs, cfg):
    x_refs, (m_ref, h_ref) = refs[:-2], refs[-2:]
    m = m_ref[...]
    h_ref[...] = (_load_rows(cfg, x_refs) * (1.0 + m[1:2]) + m[0:1]).astype(h_ref.dtype)


def modulate_first(cfg, x, mod, layer, h_dtype):
    RT, D = cfg.RT, cfg.D
    x_specs, x_args = _split_rows(cfg, x)
    return pl.pallas_call(
        functools.partial(_modulate_kernel, cfg=cfg),
        grid=(cfg.T // RT,),
        in_specs=x_specs + [_mod_spec(cfg, layer)],
        out_specs=pl.BlockSpec((RT, D), lambda i: (i, 0)),
        out_shape=jax.ShapeDtypeStruct((cfg.T, D), h_dtype),
        compiler_params=_params(("parallel",), 32),
        name="modulate_first",
    )(*x_args, mod)


def _deepnorm_ln(x, f, gate, g, b, alpha):
    y = alpha * x + gate * f
    mu = jnp.mean(y, axis=-1, keepdims=True)
    yc = y - mu
    var = jnp.mean(yc * yc, axis=-1, keepdims=True)
    return yc * lax.rsqrt(var + LN_EPS) * g + b


def _ln_router_kernel(*refs, cfg):
    x_refs = refs[:-13]
    (f_ref, m_ref, g_ref, b_ref, whi_ref, wlo_ref, br_ref,
     xo_ref, ho_ref, ridx_ref, rprob_ref, cnt_ref, carry_ref) = refs[-13:]
    i = pl.program_id(0)
    RT, n_exp, alpha = cfg.RT, cfg.E, cfg.alpha

    @pl.when(i == 0)
    def _():
        carry_ref[...] = jnp.zeros_like(carry_ref)

    m = m_ref[...]
    xn = _deepnorm_ln(_load_rows(cfg, x_refs), f_ref[...], m[2:3], g_ref[...], b_ref[...], alpha)
    xo_ref[...] = xn
    h = xn * (1.0 + m[4:5]) + m[3:4]
    ho_ref[...] = h
    h_hi = h.astype(BF16)
    h_lo = (h - h_hi.astype(F32)).astype(BF16)
    w_hi = whi_ref[...]
    logits = (jnp.dot(h_hi, w_hi, preferred_element_type=F32)
              + jnp.dot(h_lo, w_hi, preferred_element_type=F32)
              + jnp.dot(h_hi, wlo_ref[...], preferred_element_type=F32)) + br_ref[...]
    lane = lax.broadcasted_iota(I32, (RT, LANES), 1).astype(F32)
    neg = jnp.float32(-jnp.inf)
    l = jnp.where(lane < n_exp, logits, neg)
    vals, idxs = [], []
    multi = jnp.zeros((RT, LANES), F32)
    for _ in range(TOP_K):
        mx = jnp.max(l, axis=-1, keepdims=True)
        ik = jnp.min(jnp.where(l == mx, lane, float(LANES)), axis=-1, keepdims=True)
        sel = lane == ik
        vals.append(mx)
        idxs.append(ik)
        multi = multi + sel.astype(F32)
        l = jnp.where(sel, neg, l)
    exps = [jnp.exp(v - vals[0]) for v in vals]
    den = exps[0]
    for e in exps[1:]:
        den = den + e
    r_i = lax.broadcasted_iota(I32, (RT, RT), 0)
    c_i = lax.broadcasted_iota(I32, (RT, RT), 1)
    tri = (c_i < r_i).astype(BF16)
    prefix = jnp.dot(tri, multi.astype(BF16), preferred_element_type=F32) + carry_ref[0:1, :]
    ridx = jnp.zeros((RT, LANES), F32)
    rprob = jnp.zeros((RT, LANES), F32)
    for k in range(TOP_K):
        rank = jnp.sum(jnp.where(lane == idxs[k], prefix, 0.0), axis=-1, keepdims=True)
        ridx = jnp.where(lane == k, idxs[k], ridx)
        ridx = jnp.where(lane == TOP_K + k, rank, ridx)
        rprob = jnp.where(lane == k, exps[k] / den, rprob)
    ridx_ref[...] = ridx.astype(I32)
    rprob_ref[...] = rprob
    carry = carry_ref[0:1, :] + jnp.sum(multi, axis=0, keepdims=True)
    carry_ref[...] = jnp.broadcast_to(carry, carry_ref.shape)
    cnt_ref[...] = jnp.broadcast_to(carry, cnt_ref.shape).astype(I32)


def ln_router(cfg, x, f, mod, layer, ln_g, ln_b, w_router, b_router):
    RT, D, T, E = cfg.RT, cfg.D, cfg.T, cfg.E
    wr = jnp.zeros((D, LANES), F32).at[:, :E].set(w_router[layer])
    w_hi = wr.astype(BF16)
    w_lo = (wr - w_hi.astype(F32)).astype(BF16)
    br = jnp.zeros((1, LANES), F32).at[0, :E].set(b_router[layer])
    row = lambda i: (i, 0)
    const = lambda i: (0, 0)
    x_specs, x_args = _split_rows(cfg, x)
    return pl.pallas_call(
        functools.partial(_ln_router_kernel, cfg=cfg),
        grid=(T // RT,),
        in_specs=x_specs + [pl.BlockSpec((RT, D), row), _mod_spec(cfg, layer),
                  pl.BlockSpec((None, None, 1, D), lambda i: (layer, 0, 0, 0)),
                  pl.BlockSpec((None, None, 1, D), lambda i: (layer, 0, 0, 0)),
                  pl.BlockSpec((D, LANES), const), pl.BlockSpec((D, LANES), const),
                  pl.BlockSpec((1, LANES), const)],
        out_specs=[pl.BlockSpec((RT, D), row), pl.BlockSpec((RT, D), row),
                   pl.BlockSpec((RT, LANES), row), pl.BlockSpec((RT, LANES), row),
                   pl.BlockSpec((SUBLANES, LANES), const)],
        out_shape=[jax.ShapeDtypeStruct((T, D), F32), jax.ShapeDtypeStruct((T, D), F32),
                   jax.ShapeDtypeStruct((T, LANES), I32), jax.ShapeDtypeStruct((T, LANES), F32),
                   jax.ShapeDtypeStruct((SUBLANES, LANES), I32)],
        scratch_shapes=[pltpu.VMEM((SUBLANES, LANES), F32)],
        compiler_params=_params(("arbitrary",), 48),
        name="ln_router",
    )(*x_args, f, mod, ln_g.reshape(cfg.depth, 2, 1, D), ln_b.reshape(cfg.depth, 2, 1, D), w_hi, w_lo, br)


def _ln_combine_kernel(pos_cur, pos_nxt, x_ref, p_ref, m_ref, g_ref, b_ref, *rest,
                       cfg, n_tiles, has_next):
    if has_next:
        mn_ref, y_hbm, xo_ref, ho_ref, buf, sem = rest
    else:
        y_hbm, xo_ref, xs_ref, buf, sem = rest
    i = pl.program_id(0)
    CT = x_ref.shape[0]
    rows = TOP_K * CT
    slot = i % 2

    def row_copy(src_row, dst_row, s):
        return pltpu.make_async_copy(y_hbm.at[pl.ds(src_row, 1)], buf.at[pl.ds(dst_row, 1)], sem.at[s])

    def wait(s):
        def body(j, c):
            row_copy(0, s * rows, s).wait()
            return c
        lax.fori_loop(0, rows, body, 0, unroll=8)

    @pl.when(i == 0)
    def _():
        def body(j, c):
            row_copy(pos_cur[j], j, 0).start()
            return c
        lax.fori_loop(0, rows, body, 0, unroll=8)

    wait(slot)
    p = p_ref[...]
    base = pl.multiple_of(slot * rows, rows)
    f = jnp.zeros(x_ref.shape, F32)
    for k in range(TOP_K):
        f = f + p[:, k:k + 1] * buf[pl.ds(base + k * CT, CT), :]
    nxt = (1 - slot) * rows
    for j in range(rows):
        row_copy(pos_nxt[j], nxt + j, 1 - slot).start()
    m = m_ref[...]
    xn = _deepnorm_ln(x_ref[...], f, m[5:6], g_ref[...], b_ref[...], cfg.alpha)
    if has_next:
        xo_ref[...] = xn
        mn = mn_ref[...]
        ho_ref[...] = (xn * (1.0 + mn[1:2]) + mn[0:1]).astype(ho_ref.dtype)
    else:
        n_first = cfg.T_p // CT

        @pl.when(i < n_first)
        def _():
            xo_ref[...] = xn

        @pl.when(i >= n_first)
        def _():
            xs_ref[...] = xn

    @pl.when(i == n_tiles - 1)
    def _():
        wait(1 - slot)


def ln_combine(cfg, x, y, pos_flat, rprob, mod, layer, ln_g, ln_b, next_dtype):
    CT, D, T = cfg.CT, cfg.D, cfg.T
    n_tiles = T // CT
    per = cfg.RT // CT
    rows = TOP_K * CT
    has_next = next_dtype is not None
    row = lambda i: (i, 0)
    mod_cur = pl.BlockSpec((None, None, N_MOD, D), lambda i: (layer, cfg.cond_row(i // per), 0, 0))
    in_specs = [pl.BlockSpec((rows,), lambda i: (i,), memory_space=pltpu.SMEM),
                pl.BlockSpec((rows,), lambda i: (jnp.minimum(i + 1, n_tiles - 1),), memory_space=pltpu.SMEM),
                pl.BlockSpec((CT, D), row), pl.BlockSpec((CT, LANES), row), mod_cur,
                pl.BlockSpec((None, None, 1, D), lambda i: (layer, 1, 0, 0)),
                pl.BlockSpec((None, None, 1, D), lambda i: (layer, 1, 0, 0))]
    args = [pos_flat, pos_flat, x, rprob, mod, ln_g.reshape(cfg.depth, 2, 1, D), ln_b.reshape(cfg.depth, 2, 1, D)]
    if has_next:
        in_specs.append(pl.BlockSpec((None, None, N_MOD, D),
                                     lambda i: (layer + 1, cfg.cond_row(i // per), 0, 0)))
        args.append(mod)
        out_specs = [pl.BlockSpec((CT, D), row), pl.BlockSpec((CT, D), row)]
        out_shape = [jax.ShapeDtypeStruct((T, D), F32), jax.ShapeDtypeStruct((T, D), next_dtype)]
    else:
        n_first = cfg.T_p // CT
        out_specs = [pl.BlockSpec((CT, D), lambda i: (jnp.minimum(i, n_first - 1), 0)),
                     pl.BlockSpec((CT, D), lambda i: (jnp.maximum(i - n_first, 0), 0))]
        out_shape = [jax.ShapeDtypeStruct((cfg.T_p, D), F32), jax.ShapeDtypeStruct((cfg.T_s, D), F32)]
    in_specs.append(pl.BlockSpec(memory_space=pl.ANY))
    args.append(y)
    out = pl.pallas_call(
        functools.partial(_ln_combine_kernel, cfg=cfg, n_tiles=n_tiles, has_next=has_next),
        grid=(n_tiles,),
        in_specs=in_specs, out_specs=out_specs, out_shape=out_shape,
        scratch_shapes=[pltpu.VMEM((2 * rows, D), F32), pltpu.SemaphoreType.DMA((2,))],
        compiler_params=_params(("arbitrary",), 48),
        name="ln_combine",
    )(*args)
    return out[0], out[1]


def _mm_kernel(*refs, has_scale, n_first):
    a_refs, refs = (refs[:1], refs[1:]) if n_first is None else (refs[:2], refs[2:])
    if has_scale:
        w_ref, b_ref, s_ref, o_ref, wbf = refs
    else:
        w_ref, b_ref, o_ref, wbf = refs
    i = pl.program_id(2)

    @pl.when(i == 0)
    def _():
        wbf[...] = w_ref[...].astype(BF16)

    def emit(a_ref):
        acc = jnp.dot(a_ref[...], wbf[...], preferred_element_type=F32) + b_ref[...]
        if has_scale:
            acc = acc * s_ref[...]
        o_ref[...] = acc.astype(o_ref.dtype)

    if n_first is None:
        emit(a_refs[0])
    else:
        pl.when(i < n_first)(lambda: emit(a_refs[0]))
        pl.when(i >= n_first)(lambda: emit(a_refs[1]))


def matmul(cfg, a, w4, layer, bias, scale=None, out_dtype=F32, name="matmul"):
    _, G, K, N = w4.shape
    tm, tn = cfg.mm_tile, min(cfg.col_tile, N)
    n_a = 2 if isinstance(a, tuple) else 1
    out_bytes = jnp.dtype(out_dtype).itemsize
    vmem_need = lambda m: (n_a * 2 * m * K * 2 + 2 * K * tn * 4 + K * tn * 2 + 2 * m * tn * out_bytes) / MIB
    while vmem_need(tm) > MM_VMEM_LIMIT_MIB - 4 and tm > SUBLANES:
        tm //= 2
    nj = N // tn
    col = lambda g, j, i: (0, g * nj + j)
    if isinstance(a, tuple):
        n_first = a[0].shape[0] // tm
        M = a[0].shape[0] + a[1].shape[0]
        in_specs = [pl.BlockSpec((tm, K), lambda g, j, i: (jnp.minimum(i, n_first - 1), g)),
                    pl.BlockSpec((tm, K), lambda g, j, i: (jnp.maximum(i - n_first, 0), g))]
        args = list(a)
    else:
        n_first = None
        M = a.shape[0]
        in_specs = [pl.BlockSpec((tm, K), lambda g, j, i: (i, g))]
        args = [a]
    in_specs += [pl.BlockSpec((None, None, K, tn), lambda g, j, i: (layer, g, 0, j)),
                 pl.BlockSpec((1, tn), col)]
    args += [w4, bias]
    if scale is not None:
        in_specs.append(pl.BlockSpec((1, tn), col))
        args.append(scale)
    return pl.pallas_call(
        functools.partial(_mm_kernel, has_scale=scale is not None, n_first=n_first),
        grid=(G, nj, M // tm),
        in_specs=in_specs,
        out_specs=pl.BlockSpec((tm, tn), lambda g, j, i: (i, g * nj + j)),
        out_shape=jax.ShapeDtypeStruct((M, G * N), out_dtype),
        scratch_shapes=[pltpu.VMEM((K, tn), BF16)],
        compiler_params=_params(("parallel", "parallel", "arbitrary"), MM_VMEM_LIMIT_MIB),
        name=name,
    )(*args)


def _hyconv_kernel(*refs, cfg):
    z_refs, w_refs, b_refs = refs[:9], refs[9:12], refs[12:15]
    u_ref, x0_ref = refs[15:17]
    i = pl.program_id(0)
    RT = cfg.RT
    first, last, _ = cfg.seq_edges(i)
    row = lax.broadcasted_iota(I32, u_ref.shape, 0)

    def conv(zc, zp, zn, w, b):
        z = zc[...]
        prev_row = jnp.where(first, 0.0, zp[SUBLANES - 1:SUBLANES, :])
        next_row = jnp.where(last, 0.0, zn[0:1, :])
        z_m = jnp.where(row == 0, prev_row, pltpu.roll(z, 1, 0))
        z_p = jnp.where(row == RT - 1, next_row, pltpu.roll(z, RT - 1, 0))
        wv = w[...]
        return wv[0:1] * z_m + wv[1:2] * z + wv[2:3] * z_p + b[...]

    x0, x1, v = [conv(*z_refs[3 * g:3 * g + 3], w_refs[g], b_refs[g]) for g in range(3)]
    u_ref[...] = v * x1
    x0_ref[...] = x0


def hyena_short_conv(cfg, z, conv_w, conv_b, j):
    RT, D, T, tc = cfg.RT, cfg.D, cfg.T, cfg.col_tile
    nc = D // tc
    r8 = RT // SUBLANES
    in_specs, args = [], []
    for g in range(3):
        in_specs += [pl.BlockSpec((RT, tc), lambda i, c, g=g: (i, g * nc + c)),
                     pl.BlockSpec((SUBLANES, tc), lambda i, c, g=g: (jnp.maximum(i * r8 - 1, 0), g * nc + c)),
                     pl.BlockSpec((SUBLANES, tc),
                                  lambda i, c, g=g: (jnp.minimum((i + 1) * r8, T // SUBLANES - 1), g * nc + c))]
        args += [z, z, z]
    for g in range(3):
        in_specs.append(pl.BlockSpec((None, SHORT_CONV, tc), lambda i, c, g=g: (j, 0, g * nc + c)))
        args.append(conv_w)
    for g in range(3):
        in_specs.append(pl.BlockSpec((None, 1, tc), lambda i, c, g=g: (j, 0, g * nc + c)))
        args.append(conv_b.reshape(conv_b.shape[0], 1, 3 * D))
    blk = pl.BlockSpec((RT, tc), lambda i, c: (i, c))
    return pl.pallas_call(
        functools.partial(_hyconv_kernel, cfg=cfg),
        grid=(T // RT, nc),
        in_specs=in_specs, out_specs=[blk, blk],
        out_shape=[jax.ShapeDtypeStruct((T, D), F32), jax.ShapeDtypeStruct((T, D), F32)],
        compiler_params=_params(("parallel", "parallel"), 32),
        name="hyena_short_conv",
    )(*args)


def _filter_kernel(z_ref, t_ref, sg_ref, w1_ref, w23_ref, b_ref, fr_ref, wf_ref, wb_ref, dl_ref,
                   x1_ref, x2_ref, kl_ref):
    b = b_ref[...]
    fr = fr_ref[...]
    hdn = jnp.sin(fr[0:1] * (jnp.dot(z_ref[...], w1_ref[...], precision=HIGHEST,
                                     preferred_element_type=F32) + b[0:1]))
    for s in range(2):
        hdn = jnp.sin(fr[s + 1:s + 2] * (jnp.dot(hdn, w23_ref[s], precision=HIGHEST,
                                                 preferred_element_type=F32) + b[s + 1:s + 2]))
    t = t_ref[...]
    win = jnp.exp(-t * dl_ref[...]) + HYENA_DECAY_SHIFT
    hf = jnp.dot(hdn, wf_ref[...], precision=HIGHEST, preferred_element_type=F32) * win
    hb = jnp.dot(hdn, wb_ref[...], precision=HIGHEST, preferred_element_type=F32) * win
    row = lax.broadcasted_iota(I32, hb.shape, 0)
    hb = jnp.where(row == 0, 0.0, hb)
    norm = jnp.sum(jnp.abs(hf), axis=0, keepdims=True) + jnp.sum(jnp.abs(hb), axis=0, keepdims=True)
    x1 = (hf + hb) / norm
    x2 = (hb - hf) / norm
    x1_ref[...] = x1.astype(x1_ref.dtype)
    x2_ref[...] = x2.astype(x2_ref.dtype)
    kl_ref[...] = jnp.sum(sg_ref[...] * x1, axis=0, keepdims=True)


def hyena_position_features(L):
    t = jnp.linspace(0.0, 1.0, L, dtype=F32)[:, None]
    bands = (HYENA_EMB_DIM - 1) // 2
    w = 2.0 * math.pi * jnp.arange(L, dtype=F32)[:, None] / L
    f = jnp.linspace(1e-4, bands - 1, bands, dtype=F32)[None, :]
    return t, jnp.concatenate([t, jnp.cos(f * w), -jnp.sin(f * w)], axis=-1)


def hyena_filter_parts(cfg, L, j, w1, w23, b, freq, w_out):
    D, tc = cfg.D, cfg.filt_tile
    width = w1.shape[-1]
    t, z = hyena_position_features(L)
    zp = jnp.zeros((L, LANES), F32).at[:, :HYENA_EMB_DIM].set(z)
    w1p = jnp.zeros((LANES, width), F32).at[:HYENA_EMB_DIM].set(w1[j])
    sign = (1 - 2 * (jnp.arange(L) % 2)).astype(F32)[:, None]
    min_decay = math.log(HYENA_DECAY_TARGET) / HYENA_SLOW_DECAY
    max_decay = math.log(HYENA_DECAY_TARGET) / HYENA_FAST_DECAY
    deltas = jnp.abs(jnp.linspace(min_decay, max_decay, D, dtype=F32))[None, :]
    nc = D // tc
    const2 = lambda c: (0, 0)
    return pl.pallas_call(
        _filter_kernel,
        grid=(nc,),
        in_specs=[pl.BlockSpec((L, LANES), const2), pl.BlockSpec((L, 1), const2), pl.BlockSpec((L, 1), const2),
                  pl.BlockSpec((LANES, width), const2),
                  pl.BlockSpec((None, 2, width, width), lambda c: (j, 0, 0, 0)),
                  pl.BlockSpec((None, 3, width), lambda c: (j, 0, 0)),
                  pl.BlockSpec((None, 3, width), lambda c: (j, 0, 0)),
                  pl.BlockSpec((None, width, tc), lambda c: (j, 0, c)),
                  pl.BlockSpec((None, width, tc), lambda c: (j, 0, nc + c)),
                  pl.BlockSpec((1, tc), lambda c: (0, c))],
        out_specs=[pl.BlockSpec((L, tc), lambda c: (0, c)), pl.BlockSpec((L, tc), lambda c: (0, c)),
                   pl.BlockSpec((1, tc), lambda c: (0, c))],
        out_shape=[jax.ShapeDtypeStruct((L, D), BF16), jax.ShapeDtypeStruct((L, D), BF16),
                   jax.ShapeDtypeStruct((1, D), F32)],
        compiler_params=_params(("parallel",), 48),
        name="hyena_filter",
    )(zp, t, sign, w1p, w23, b, freq, w_out, w_out, deltas)


def dft_matrices(L):
    f = jnp.arange(L, dtype=I32)[:, None]
    t = jnp.arange(L, dtype=I32)[None, :]
    ang = ((f * t) % (2 * L)).astype(F32) * (math.pi / L)
    c, s = jnp.cos(ang), jnp.sin(ang)
    sign = (1 - 2 * (t % 2)).astype(F32)
    fwd = jnp.concatenate([c, jnp.where(f == 0, sign, s)], axis=0)
    ga = c * jnp.where(f == 0, 0.5, 1.0) / L
    gb = jnp.where(f == 0, sign / (2 * L), -s / L)
    inv = jnp.concatenate([ga.T, gb.T], axis=1)
    return fwd.astype(BF16), inv.astype(BF16)


def _kf_kernel(f_ref, x1_ref, x2_ref, kl_ref, o_ref, *, nk):
    k2 = pl.program_id(1)

    @pl.when(k2 < nk)
    def _():
        o_ref[...] = jnp.dot(f_ref[...], x1_ref[...], preferred_element_type=F32)

    @pl.when(k2 >= nk)
    def _():
        o_ref[...] = jnp.dot(f_ref[...], x2_ref[...], preferred_element_type=F32)

    @pl.when(k2 == nk)
    def _():
        o_ref[0:1, :] = kl_ref[...]


def hyena_filter_spectrum(cfg, L, fwd, x1, x2, kl):
    D, tc = cfg.D, cfg.col_tile
    tf = min(cfg.freq_tile, L)
    nk = L // tf
    return pl.pallas_call(
        functools.partial(_kf_kernel, nk=nk),
        grid=(D // tc, 2 * nk),
        in_specs=[pl.BlockSpec((tf, L), lambda c, k: (k, 0)),
                  pl.BlockSpec((L, tc), lambda c, k: (0, c)), pl.BlockSpec((L, tc), lambda c, k: (0, c)),
                  pl.BlockSpec((1, tc), lambda c, k: (0, c))],
        out_specs=pl.BlockSpec((tf, tc), lambda c, k: (k, c)),
        out_shape=jax.ShapeDtypeStruct((2 * L, D), F32),
        compiler_params=_params(("parallel", "arbitrary"), 32),
        name="hyena_filter_spectrum",
    )(fwd, x1, x2, kl)


def _lconv_kernel(u_ref, x0_ref, fb_ref, fa_ref, fs_ref, ga_ref, gs_ref, ka_ref, ks_ref, o_ref, acc,
                  *, L, n_seq, nk):
    k = pl.program_id(2)

    @pl.when(k == 0)
    def _():
        acc[...] = jnp.zeros_like(acc)

    ka, ks = ka_ref[...], ks_ref[...]
    row0 = (lax.broadcasted_iota(I32, ka.shape, 0) == 0) & (k == 0)
    for s in range(n_seq):
        rows = pl.ds(s * L, L)
        us = u_ref[rows, :].astype(BF16)
        a = jnp.dot(fa_ref[...], us, preferred_element_type=F32)
        b = jnp.dot(fs_ref[...], us, preferred_element_type=F32)
        bk = b * ks
        ya = a * ka + jnp.where(row0, 0.0, bk)
        ys = jnp.where(row0, bk, a * ks - b * ka)
        acc[rows, :] += (jnp.dot(ga_ref[...], ya.astype(BF16), preferred_element_type=F32)
                         + jnp.dot(gs_ref[...], ys.astype(BF16), preferred_element_type=F32))

    @pl.when(k == nk - 1)
    def _():
        o_ref[...] = ((acc[...] + u_ref[...] * fb_ref[...]) * x0_ref[...]).astype(o_ref.dtype)


def hyena_long_conv(cfg, L, row_off, n_rows, u, x0, filter_bias, j, fwd, inv, kf):
    D, tc = cfg.D, cfg.col_tile
    RB = max(L, cfg.conv_rows)
    n_seq = RB // L
    tf = min(cfg.freq_tile, L)
    nk = L // tf
    off = row_off // RB
    blk = pl.BlockSpec((RB, tc), lambda r, c, k: (off + r, c))
    in_specs = [blk, blk,
                pl.BlockSpec((None, 1, tc), lambda r, c, k: (j, 0, c)),
                pl.BlockSpec((tf, L), lambda r, c, k: (k, 0)), pl.BlockSpec((tf, L), lambda r, c, k: (nk + k, 0)),
                pl.BlockSpec((L, tf), lambda r, c, k: (0, k)), pl.BlockSpec((L, tf), lambda r, c, k: (0, nk + k)),
                pl.BlockSpec((tf, tc), lambda r, c, k: (k, c)), pl.BlockSpec((tf, tc), lambda r, c, k: (nk + k, c))]
    args = [u, x0, filter_bias.reshape(filter_bias.shape[0], 1, D), fwd, fwd, inv, inv, kf, kf]
    return pl.pallas_call(
        functools.partial(_lconv_kernel, L=L, n_seq=n_seq, nk=nk),
        grid=(n_rows // RB, D // tc, nk),
        in_specs=in_specs, out_specs=pl.BlockSpec((RB, tc), lambda r, c, k: (r, c)),
        out_shape=jax.ShapeDtypeStruct((n_rows, D), BF16),
        scratch_shapes=[pltpu.VMEM((RB, tc), F32)],
        compiler_params=_params(("parallel", "parallel", "arbitrary"), 48),
        name="hyena_long_conv",
    )(*args)


def hyena_mixer(cfg, h, j, hy_w_in, hy_b_in, hy_conv_w, hy_conv_b, filt, hy_filter_bias, hy_w_out, hy_b_out):
    D = cfg.D
    z = matmul(cfg, h, hy_w_in.reshape(hy_w_in.shape[0], 1, D, 3 * D), j, hy_b_in[j][None, :], name="hyena_in")
    u, x0 = hyena_short_conv(cfg, z, hy_conv_w, hy_conv_b, j)
    out = []
    for L, row_off, n_rows in ((cfg.seq, 0, cfg.T_p), (cfg.dec_seq, cfg.T_p, cfg.T_s)):
        x1, x2, kl = hyena_filter_parts(cfg, L, j, *filt)
        fwd, inv = dft_matrices(L)
        kf = hyena_filter_spectrum(cfg, L, fwd, x1, x2, kl)
        out.append(hyena_long_conv(cfg, L, row_off, n_rows, u, x0, hy_filter_bias, j, fwd, inv, kf))
    return matmul(cfg, tuple(out), hy_w_out.reshape(hy_w_out.shape[0], 1, D, D), j, hy_b_out[j][None, :], name="hyena_out")


def _qk_post_kernel(x_ref, g_ref, cos_ref, sin_ref, o_ref, *rest, cfg, emit_norm):
    i = pl.program_id(0)
    hd = cfg.hd
    g = g_ref[...]
    lane = lax.broadcasted_iota(I32, (x_ref.shape[0], hd), 1)
    low = (lane % (hd // 2)) < (hd // 4)
    is_latent = i >= cfg.NP_T
    for hh in range(x_ref.shape[1] // hd):
        cols = slice(hh * hd, (hh + 1) * hd)
        x = x_ref[:, cols]
        xn = x * lax.rsqrt(jnp.mean(x * x, axis=-1, keepdims=True) + RMS_EPS) * g
        if emit_norm:
            rest[0][:, cols] = xn
        swapped = jnp.where(low, pltpu.roll(xn, hd - hd // 4, 1), pltpu.roll(xn, hd // 4, 1))
        roped = xn * cos_ref[...] + swapped * sin_ref[...]
        o_ref[:, cols] = jnp.where(is_latent, roped, xn).astype(o_ref.dtype)


def axial_rope_tables(cfg):
    L, hd = cfg.dec_seq, cfg.hd
    axis_dim = hd // 2
    rows = L // cfg.grid_w
    row = jnp.repeat(jnp.arange(rows), cfg.grid_w)
    col = jnp.tile(jnp.arange(cfg.grid_w), rows)
    inv_freq = ROPE_THETA ** (-jnp.arange(0, axis_dim, 2, dtype=F32) / axis_dim)
    ang = jnp.stack([row, col], axis=-1).astype(F32)[..., None] * inv_freq
    cos, sin = jnp.cos(ang), jnp.sin(ang)
    cos_t = jnp.concatenate([cos, cos], axis=-1).reshape(L, hd)
    sin_t = jnp.concatenate([-sin, sin], axis=-1).reshape(L, hd)
    return cos_t, sin_t


def qk_post(cfg, qkv, norm_w, j, cos_t, sin_t, col_off, n_cols, emit_norm):
    RT, T, hd = cfg.RT, cfg.T, cfg.hd
    tc = min(cfg.col_tile, n_cols)
    blk_off = col_off // tc
    pos_blk = lambda i, c: (jnp.where(i < cfg.NP_T, 0, (i - cfg.NP_T) % cfg.TPS), 0)
    out_specs = [pl.BlockSpec((RT, tc), lambda i, c: (i, c))]
    out_shape = [jax.ShapeDtypeStruct((T, n_cols), BF16)]
    if emit_norm:
        out_specs.append(pl.BlockSpec((RT, tc), lambda i, c: (i, c)))
        out_shape.append(jax.ShapeDtypeStruct((T, n_cols), F32))
    return pl.pallas_call(
        functools.partial(_qk_post_kernel, cfg=cfg, emit_norm=emit_norm),
        grid=(T // RT, n_cols // tc),
        in_specs=[pl.BlockSpec((RT, tc), lambda i, c: (i, blk_off + c)),
                  pl.BlockSpec((None, 1, hd), lambda i, c: (j, 0, 0)),
                  pl.BlockSpec((RT, hd), pos_blk), pl.BlockSpec((RT, hd), pos_blk)],
        out_specs=out_specs, out_shape=out_shape,
        compiler_params=_params(("parallel", "parallel"), 32),
        name="qk_norm_rope",
    )(qkv, norm_w.reshape(norm_w.shape[0], 1, hd), cos_t, sin_t)


def _attn_kernel(*refs, cfg, has_cache):
    if has_cache:
        q_ref, k_ref, v_ref, kc_ref, vc_ref, o_ref = refs
    else:
        q_ref, k_ref, v_ref, o_ref = refs
    hd = cfg.hd
    scale = hd ** -0.5
    nt = (((1,), (1,)), ((), ()))
    k = k_ref[...]
    v = v_ref[...].astype(BF16)
    if has_cache:
        kc = kc_ref[...].astype(BF16)
        vc = vc_ref[...].astype(BF16)
    for g in range(cfg.gqa):
        cols = slice(g * hd, (g + 1) * hd)
        q = q_ref[:, cols]
        s = lax.dot_general(q, k, nt, preferred_element_type=F32) * scale
        m = jnp.max(s, axis=-1, keepdims=True)
        if has_cache:
            sc = lax.dot_general(q, kc, nt, preferred_element_type=F32) * scale
            m = jnp.maximum(m, jnp.max(sc, axis=-1, keepdims=True))
        p = jnp.exp(s - m)
        l = jnp.sum(p, axis=-1, keepdims=True)
        o = jnp.dot(p.astype(BF16), v, preferred_element_type=F32)
        if has_cache:
            pc = jnp.exp(sc - m)
            l = l + jnp.sum(pc, axis=-1, keepdims=True)
            o = o + jnp.dot(pc.astype(BF16), vc, preferred_element_type=F32)
        o_ref[:, cols] = (o / l).astype(o_ref.dtype)


def attention(cfg, q, k, qkv, n_seq, L, row_off, cache):
    RT, D, hd, G = cfg.RT, cfg.D, cfg.hd, cfg.gqa
    v_col = cfg.n_heads + cfg.n_kv
    q_off, kv_off, nq = row_off // RT, row_off // L, L // RT
    in_specs = [pl.BlockSpec((RT, G * hd), lambda b, h, t: (q_off + b * nq + t, h)),
                pl.BlockSpec((L, hd), lambda b, h, t: (kv_off + b, h)),
                pl.BlockSpec((L, hd), lambda b, h, t: (kv_off + b, v_col + h))]
    args = [q, k, qkv]
    if cache is not None:
        ck, cv, n_attn, j = cache
        past = ck.shape[0] // (n_seq * n_attn)
        spec = pl.BlockSpec((past, hd), lambda b, h, t: (b * n_attn + j, h))
        in_specs += [spec, spec]
        args += [ck, cv]
    return pl.pallas_call(
        functools.partial(_attn_kernel, cfg=cfg, has_cache=cache is not None),
        grid=(n_seq, cfg.n_kv, nq),
        in_specs=in_specs,
        out_specs=pl.BlockSpec((RT, G * hd), lambda b, h, t: (b * nq + t, h)),
        out_shape=jax.ShapeDtypeStruct((n_seq * L, D), BF16),
        compiler_params=_params(("parallel", "parallel", "parallel"), 48),
        name="attention",
    )(*args)


def attention_mixer(cfg, h, j, cache_k, cache_v, at_w_qkv, at_q_norm, at_k_norm, at_w_o):
    D, hd = cfg.D, cfg.hd
    n_attn = at_w_qkv.shape[0]
    qkv_dim = at_w_qkv.shape[-1]
    kv_dim = cfg.n_kv * hd
    qkv = matmul(cfg, h, at_w_qkv.reshape(n_attn, 1, D, qkv_dim), j, jnp.zeros((1, qkv_dim), F32), name="attn_qkv")
    cos_t, sin_t = axial_rope_tables(cfg)
    (q,) = qk_post(cfg, qkv, at_q_norm, j, cos_t, sin_t, 0, D, False)
    k, k_norm = qk_post(cfg, qkv, at_k_norm, j, cos_t, sin_t, D, kv_dim, True)
    o_p = attention(cfg, q, k, qkv, cfg.batch, cfg.seq, 0, None)
    ck = cache_k.reshape(-1, kv_dim)
    cv = cache_v.reshape(-1, kv_dim)
    o_s = attention(cfg, q, k, qkv, cfg.dec_batch, cfg.dec_seq, cfg.T_p, (ck, cv, n_attn, j))
    f = matmul(cfg, (o_p, o_s), at_w_o.reshape(n_attn, 1, D, D), j, jnp.zeros((1, D), F32), name="attn_out")
    new_k = k_norm[:cfg.T_p].reshape(cfg.batch, cfg.seq, cfg.n_kv, hd)
    new_v = qkv[:cfg.T_p, D + kv_dim:].reshape(cfg.batch, cfg.seq, cfg.n_kv, hd)
    return f, new_k, new_v


def _pool_kernel(hc, hp, hn, o_ref, ext, *, cfg, cols_per_group):
    i = pl.program_id(0)
    c = pl.program_id(1)
    RT = cfg.RT
    H = SUBLANES
    first, last, p0 = cfg.seq_edges(i)
    L = jnp.where(i < cfg.NP_T, cfg.seq, cfg.dec_seq)
    ext[0:H, :] = jnp.where(first, 0.0, hp[...])
    ext[H:H + RT, :] = hc[...]
    ext[H + RT:, :] = jnp.where(last, 0.0, hn[...])
    pos = p0 + lax.broadcasted_iota(I32, (RT, 1), 0)
    for gi, w in enumerate(POOL_WINDOWS):
        @pl.when(c // cols_per_group == gi)
        def _(w=w):
            half = w // 2
            s = ext[H - half:H - half + RT, :]
            for o in range(1 - half, half):
                s = s + ext[H + o:H + o + RT, :]
            lo = jnp.clip(pos - half, 0, L)
            hi = jnp.clip(pos - half + w, 0, L)
            mean = s / (hi - lo).astype(F32)
            o_ref[...] = (mean - hc[...]).astype(o_ref.dtype)


def pool_mixer(cfg, h, j, pl_w, pl_b, pl_scale):
    RT, D, T = cfg.RT, cfg.D, cfg.T
    gd = D // cfg.n_pool
    tc = min(cfg.col_tile, gd)
    r8 = RT // SUBLANES
    assert max(POOL_WINDOWS) // 2 <= SUBLANES
    d = pl.pallas_call(
        functools.partial(_pool_kernel, cfg=cfg, cols_per_group=gd // tc),
        grid=(T // RT, D // tc),
        in_specs=[pl.BlockSpec((RT, tc), lambda i, c: (i, c)),
                  pl.BlockSpec((SUBLANES, tc), lambda i, c: (jnp.maximum(i * r8 - 1, 0), c)),
                  pl.BlockSpec((SUBLANES, tc), lambda i, c: (jnp.minimum((i + 1) * r8, T // SUBLANES - 1), c))],
        out_specs=pl.BlockSpec((RT, tc), lambda i, c: (i, c)),
        out_shape=jax.ShapeDtypeStruct((T, D), BF16),
        scratch_shapes=[pltpu.VMEM((RT + 2 * SUBLANES, tc), F32)],
        compiler_params=_params(("parallel", "parallel"), 32),
        name="pool_window",
    )(h, h, h)
    return matmul(cfg, d, pl_w, j, pl_b[j][None, :], scale=pl_scale[j][None, :], name="pool_proj")


def _cast_kernel(x_ref, o_ref):
    o_ref[...] = x_ref[...].astype(o_ref.dtype)


def cast_bf16(w, rows=1024):
    shape = w.shape
    w2 = w.reshape(-1, shape[-1])
    rows = min(rows, w2.shape[0])
    out = pl.pallas_call(
        _cast_kernel,
        grid=(w2.shape[0] // rows,),
        in_specs=[pl.BlockSpec((rows, shape[-1]), lambda i: (i, 0))],
        out_specs=pl.BlockSpec((rows, shape[-1]), lambda i: (i, 0)),
        out_shape=jax.ShapeDtypeStruct(w2.shape, BF16),
        compiler_params=_params(("parallel",), 48),
        name="cast_bf16",
    )(w2)
    return out.reshape(shape)


def _moe_up_kernel(te_ref, nu_ref, src_cur, src_nxt, h_hbm, w_ref, b_ref, o_ref, buf, xb, sem, *, F, chunk):
    t = pl.program_id(0)
    n_used = nu_ref[0]
    TM = o_ref.shape[0]
    slot = t % 2

    def row_copy(src_row, dst_row, s):
        return pltpu.make_async_copy(h_hbm.at[pl.ds(src_row, 1)], buf.at[pl.ds(dst_row, 1)], sem.at[s])

    def wait(s):
        def body(r, c):
            row_copy(0, s * TM, s).wait()
            return c
        lax.fori_loop(0, TM, body, 0, unroll=8)

    @pl.when((t == 0) & (n_used > 0))
    def _():
        def body(r, c):
            row_copy(src_cur[r], r, 0).start()
            return c
        lax.fori_loop(0, TM, body, 0, unroll=8)

    @pl.when(t < n_used)
    def _():
        wait(slot)
        xb[...] = buf[pl.ds(pl.multiple_of(slot * TM, TM), TM), :].astype(BF16)
        x = xb[...]
        n_chunks = F // chunk
        per = TM // n_chunks
        nxt = (1 - slot) * TM
        for c in range(n_chunks):
            for r in range(c * per, (c + 1) * per):
                row_copy(src_nxt[r], nxt + r, 1 - slot).start()
            gc = slice(c * chunk, (c + 1) * chunk)
            uc = slice(F + c * chunk, F + (c + 1) * chunk)
            g = jnp.dot(x, w_ref[:, gc], preferred_element_type=F32) + b_ref[:, gc]
            u = jnp.dot(x, w_ref[:, uc], preferred_element_type=F32) + b_ref[:, uc]
            g = jnp.minimum(g, SWIGLU_LIMIT)
            u = jnp.clip(u, -SWIGLU_LIMIT, SWIGLU_LIMIT)
            o_ref[:, gc] = (g * jax.nn.sigmoid(SWIGLU_ALPHA * g) * (u + 1.0)).astype(o_ref.dtype)

    @pl.when(t == n_used - 1)
    def _():
        wait(1 - slot)

    @pl.when(t >= n_used)
    def _():
        o_ref[...] = jnp.zeros_like(o_ref)


def moe_up(cfg, h, src, tile_expert, n_used, w_gu_bf, b_gu, layer):
    TM, D, F, NT = cfg.TM, cfg.D, cfg.F, cfg.NT
    grid_spec = pltpu.PrefetchScalarGridSpec(
        num_scalar_prefetch=2,
        grid=(NT,),
        in_specs=[pl.BlockSpec((TM,), lambda t, te, nu: (t,), memory_space=pltpu.SMEM),
                  pl.BlockSpec((TM,), lambda t, te, nu: (jnp.minimum(t + 1, NT - 1),), memory_space=pltpu.SMEM),
                  pl.BlockSpec(memory_space=pl.ANY),
                  pl.BlockSpec((None, None, D, 2 * F), lambda t, te, nu: (layer, te[t], 0, 0)),
                  pl.BlockSpec((None, None, 1, 2 * F), lambda t, te, nu: (layer, te[t], 0, 0))],
        out_specs=pl.BlockSpec((TM, F), lambda t, te, nu: (t, 0)),
        scratch_shapes=[pltpu.VMEM((2 * TM, D), F32), pltpu.VMEM((TM, D), BF16),
                        pltpu.SemaphoreType.DMA((2,))])
    return pl.pallas_call(
        functools.partial(_moe_up_kernel, F=F, chunk=min(F, 2 * LANES)),
        grid_spec=grid_spec,
        out_shape=jax.ShapeDtypeStruct((cfg.P, F), BF16),
        compiler_params=_params(("arbitrary",), 56),
        name="moe_up",
    )(tile_expert, n_used, src, src, h, w_gu_bf, b_gu.reshape(cfg.depth, cfg.E, 1, 2 * F))


def _moe_down_kernel(te_ref, nu_ref, a_ref, w_ref, b_ref, o_ref, wbf):
    t = pl.program_id(0)

    @pl.when((t == 0) | (te_ref[t] != te_ref[jnp.maximum(t - 1, 0)]))
    def _():
        wbf[...] = w_ref[...].astype(BF16)

    @pl.when(t < nu_ref[0])
    def _():
        o_ref[...] = jnp.dot(a_ref[...], wbf[...], preferred_element_type=F32) + b_ref[...]

    @pl.when(t >= nu_ref[0])
    def _():
        o_ref[...] = jnp.zeros_like(o_ref)


def moe_down(cfg, a, tile_expert, n_used, w_down, b_down, layer):
    TM, D, F, NT = cfg.TM, cfg.D, cfg.F, cfg.NT
    grid_spec = pltpu.PrefetchScalarGridSpec(
        num_scalar_prefetch=2,
        grid=(NT,),
        in_specs=[pl.BlockSpec((TM, F), lambda t, te, nu: (t, 0)),
                  pl.BlockSpec((None, None, F, D), lambda t, te, nu: (layer, te[t], 0, 0)),
                  pl.BlockSpec((None, None, 1, D), lambda t, te, nu: (layer, te[t], 0, 0))],
        out_specs=pl.BlockSpec((TM, D), lambda t, te, nu: (t, 0)),
        scratch_shapes=[pltpu.VMEM((F, D), BF16)])
    return pl.pallas_call(
        _moe_down_kernel,
        grid_spec=grid_spec,
        out_shape=jax.ShapeDtypeStruct((cfg.P, D), F32),
        compiler_params=_params(("arbitrary",), 56),
        name="moe_down",
    )(tile_expert, n_used, a, w_down, b_down.reshape(cfg.depth, cfg.E, 1, D))


def route_metadata(cfg, ridx, cnt):
    T, E, TM, NT, CT = cfg.T, cfg.E, cfg.TM, cfg.NT, cfg.CT
    idx = ridx[:, :TOP_K]
    rank = ridx[:, TOP_K:2 * TOP_K]
    counts = cnt[0, :E]
    padded = ((counts + TM - 1) // TM) * TM
    e_i = jnp.arange(E, dtype=I32)
    ends = jnp.sum(jnp.where(e_i[None, :] <= e_i[:, None], padded[None, :], 0), axis=1)
    starts = ends - padded
    pos = rank + jnp.sum(jnp.where(idx[:, :, None] == e_i, starts, 0), axis=-1)
    token = jnp.broadcast_to(jnp.arange(T, dtype=I32)[:, None], (T, TOP_K))
    src = jnp.zeros((cfg.P,), I32).at[pos.reshape(-1)].set(token.reshape(-1), unique_indices=True)
    n_used = ends[-1] // TM
    tile_start = jnp.minimum(jnp.arange(NT, dtype=I32), n_used - 1) * TM
    te = jnp.sum((ends[None, :] <= tile_start[:, None]).astype(I32), axis=1)
    te = jnp.minimum(te, E - 1)
    pos_flat = pos.reshape(T // CT, CT, TOP_K).transpose(0, 2, 1).reshape(-1)
    return src, te, n_used.reshape(1), pos_flat


def kernel(x_prompt, x_sample, cache_k, cache_v, c, c_ctx, mod_w, mod_b, ln_g, ln_b, hy_w_in, hy_b_in, hy_conv_w, hy_conv_b, hy_ffn_w1, hy_ffn_w23, hy_ffn_b, hy_sin_freq, hy_ffn_w_out, hy_filter_bias, hy_w_out, hy_b_out, at_w_qkv, at_q_norm, at_k_norm, at_w_o, pl_w, pl_b, pl_scale, moe_w_router, moe_b_router, moe_w_gu, moe_b_gu, moe_w_down, moe_b_down):
    batch, seq, D = x_prompt.shape
    dec_batch, dec_seq, _ = x_sample.shape
    depth = mod_w.shape[0]
    cfg = Cfg(D, batch, seq, dec_batch, dec_seq, cache_k.shape[2], depth,
              moe_w_router.shape[-1], moe_w_down.shape[2])
    return trunk_step(cfg, x_prompt, x_sample, cache_k, cache_v, c, c_ctx, mod_w, mod_b, ln_g, ln_b,
                      hy_w_in, hy_b_in, hy_conv_w, hy_conv_b, hy_ffn_w1, hy_ffn_w23, hy_ffn_b, hy_sin_freq,
                      hy_ffn_w_out, hy_filter_bias, hy_w_out, hy_b_out, at_w_qkv, at_q_norm, at_k_norm, at_w_o,
                      pl_w, pl_b, pl_scale, moe_w_router, moe_b_router, moe_w_gu, moe_b_gu, moe_w_down,
                      moe_b_down)


def trunk_step(cfg, x_prompt, x_sample, cache_k, cache_v, c, c_ctx, mod_w, mod_b, ln_g, ln_b,
               hy_w_in, hy_b_in, hy_conv_w, hy_conv_b, hy_ffn_w1, hy_ffn_w23, hy_ffn_b, hy_sin_freq,
               hy_ffn_w_out, hy_filter_bias, hy_w_out, hy_b_out, at_w_qkv, at_q_norm, at_k_norm, at_w_o,
               pl_w, pl_b, pl_scale, moe_w_router, moe_b_router, moe_w_gu, moe_b_gu, moe_w_down, moe_b_down):
    D, depth = cfg.D, cfg.depth
    n_mixers = 3
    assert cfg.dec_batch + 1 <= COND_ROWS
    cond = jnp.zeros((COND_ROWS, D), F32).at[0].set(c_ctx).at[1:1 + cfg.dec_batch].set(c)
    mod = modulation_all(cfg, cond, mod_w, mod_b)
    w_gu_bf = cast_bf16(moe_w_gu)
    x = (x_prompt.reshape(cfg.T_p, D), x_sample.reshape(cfg.T_s, D))
    mixer_in_dtype = lambda i: F32 if i % n_mixers == 2 else BF16
    h = modulate_first(cfg, x, mod, 0, mixer_in_dtype(0))
    ctx_k, ctx_v = [], []
    for i in range(depth):
        kind, j = i % n_mixers, i // n_mixers
        if kind == 0:
            filt = (hy_ffn_w1, hy_ffn_w23, hy_ffn_b, hy_sin_freq, hy_ffn_w_out)
            f = hyena_mixer(cfg, h, j, hy_w_in, hy_b_in, hy_conv_w, hy_conv_b, filt,
                            hy_filter_bias, hy_w_out, hy_b_out)
        elif kind == 1:
            f, new_k, new_v = attention_mixer(cfg, h, j, cache_k, cache_v, at_w_qkv, at_q_norm, at_k_norm, at_w_o)
            ctx_k.append(new_k)
            ctx_v.append(new_v)
        else:
            f = pool_mixer(cfg, h, j, pl_w, pl_b, pl_scale)
        x, h_ffn, ridx, rprob, cnt = ln_router(cfg, x, f, mod, i, ln_g, ln_b, moe_w_router, moe_b_router)
        src, tile_expert, n_used, pos_flat = route_metadata(cfg, ridx, cnt)
        a = moe_up(cfg, h_ffn, src, tile_expert, n_used, w_gu_bf, moe_b_gu, i)
        y = moe_down(cfg, a, tile_expert, n_used, moe_w_down, moe_b_down, i)
        if i + 1 < depth:
            x, h = ln_combine(cfg, x, y, pos_flat, rprob, mod, i, ln_g, ln_b, mixer_in_dtype(i + 1))
        else:
            xp, xs = ln_combine(cfg, x, y, pos_flat, rprob, mod, i, ln_g, ln_b, None)
    xp = xp.reshape(cfg.batch, cfg.seq, D)
    xs = xs.reshape(cfg.dec_batch, cfg.dec_seq, D)
    return (xp, xs, jnp.stack(ctx_k, axis=1), jnp.stack(ctx_v, axis=1))
```

```python
import functools
import math

import jax
import jax.numpy as jnp
from jax import lax
from jax.experimental import pallas as pl
from jax.experimental.pallas import tpu as pltpu

F32 = jnp.float32
BF16 = jnp.bfloat16
I32 = jnp.int32

LANES = 128
SUBLANES = 8
VMEM_BYTES_V7X = 64 * 1024 * 1024
MIB = 1024 * 1024
MM_VMEM_LIMIT_MIB = 48

GRID_W = 64
N_HEADS = 32
N_KV_HEADS = 8
ROPE_THETA = 10000.0
RMS_EPS = 1e-6
SHORT_CONV = 3
HYENA_EMB_DIM = 33
HYENA_FAST_DECAY = 0.3
HYENA_SLOW_DECAY = 1.5
HYENA_DECAY_TARGET = 1e-2
HYENA_DECAY_SHIFT = 0.05
POOL_WINDOWS = (2, 4, 8, 16)
TOP_K = 4
SWIGLU_LIMIT = 7.0
SWIGLU_ALPHA = 1.702
LN_EPS = 1e-5
N_MOD = 6
COND_ROWS = 8
HIGHEST = lax.Precision.HIGHEST


class Cfg:
    def __init__(self, D, batch, seq, dec_batch, dec_seq, past, depth, n_exp, d_exp,
                 n_heads=N_HEADS, n_kv=N_KV_HEADS, grid_w=GRID_W,
                 mm_tile=1024, col_tile=512, ew_tile=1024, moe_tile=256, comb_tile=128, conv_rows=2048,
                 freq_tile=512, filt_tile=256):
        self.D, self.batch, self.seq = D, batch, seq
        self.dec_batch, self.dec_seq, self.past = dec_batch, dec_seq, past
        self.depth, self.E, self.F = depth, n_exp, d_exp
        self.n_heads, self.n_kv, self.grid_w = n_heads, n_kv, grid_w
        self.hd = D // n_heads
        self.gqa = n_heads // n_kv
        self.T_p = batch * seq
        self.T_s = dec_batch * dec_seq
        self.T = self.T_p + self.T_s
        self.RT = seq
        assert dec_seq % self.RT == 0 and self.RT % SUBLANES == 0
        self.NP_T = self.T_p // self.RT
        self.TPS = dec_seq // self.RT
        self.mm_tile = min(mm_tile, self.T_p)
        self.col_tile = min(col_tile, D)
        self.ew_tile = min(ew_tile, D)
        self.TM = moe_tile
        self.P = self.T * TOP_K + n_exp * self.TM
        self.NT = self.P // self.TM
        self.CT = min(comb_tile, self.RT)
        self.conv_rows = min(conv_rows, self.T_p)
        assert self.T_p % self.conv_rows == 0 and self.conv_rows % seq == 0
        assert dec_seq % self.conv_rows == 0 or self.conv_rows % dec_seq == 0
        self.freq_tile = freq_tile
        self.filt_tile = min(filt_tile, D)
        self.alpha = (2 * depth) ** 0.25
        self.n_pool = len(POOL_WINDOWS)

    def cond_row(self, i):
        return jnp.where(i < self.NP_T, 0, 1 + (i - self.NP_T) // self.TPS)

    def seq_edges(self, i):
        is_p = i < self.NP_T
        k = (i - self.NP_T) % self.TPS
        return is_p | (k == 0), is_p | (k == self.TPS - 1), jnp.where(is_p, 0, k * self.RT)


def _params(sem, vmem_mib):
    return pltpu.CompilerParams(dimension_semantics=sem, vmem_limit_bytes=int(vmem_mib * MIB))


def _silu(x):
    return x * jax.nn.sigmoid(x)


def _mod_kernel(c_ref, w_ref, b_ref, o_ref):
    s = _silu(c_ref[...]).astype(BF16)
    o_ref[...] = jnp.dot(s, w_ref[...].astype(BF16), preferred_element_type=F32) + b_ref[...]


def modulation_all(cfg, cond, mod_w, mod_b):
    depth, D, N = mod_w.shape
    tn = cfg.col_tile
    out = pl.pallas_call(
        _mod_kernel,
        grid=(depth, N // tn),
        in_specs=[pl.BlockSpec((COND_ROWS, D), lambda l, j: (0, 0)),
                  pl.BlockSpec((None, D, tn), lambda l, j: (l, 0, j)),
                  pl.BlockSpec((None, 1, tn), lambda l, j: (l, 0, j))],
        out_specs=pl.BlockSpec((None, COND_ROWS, tn), lambda l, j: (l, 0, j)),
        out_shape=jax.ShapeDtypeStruct((depth, COND_ROWS, N), F32),
        compiler_params=_params(("parallel", "parallel"), 40),
        name="modulation",
    )(cond, mod_w, mod_b.reshape(depth, 1, N))
    return out.reshape(depth, COND_ROWS, N_MOD, D)


def _mod_spec(cfg, layer):
    return pl.BlockSpec((None, None, N_MOD, cfg.D), lambda i, *_: (layer, cfg.cond_row(i), 0, 0))


def _split_rows(cfg, x):
    RT, D = cfg.RT, cfg.D
    if isinstance(x, tuple):
        return ([pl.BlockSpec((RT, D), lambda i: (jnp.minimum(i, cfg.NP_T - 1), 0)),
                 pl.BlockSpec((RT, D), lambda i: (jnp.maximum(i - cfg.NP_T, 0), 0))], list(x))
    return [pl.BlockSpec((RT, D), lambda i: (i, 0))], [x]


def _load_rows(cfg, x_refs):
    if len(x_refs) == 1:
        return x_refs[0][...]
    return jnp.where(pl.program_id(0) < cfg.NP_T, x_refs[0][...], x_refs[1][...])


def _modulate_kernel(*refs, cfg):
    x_refs, (m_ref, h_ref) = refs[:-2], refs[-2:]
    m = m_ref[...]
    h_ref[...] = (_load_rows(cfg, x_refs) * (1.0 + m[1:2]) + m[0:1]).astype(h_ref.dtype)


def modulate_first(cfg, x, mod, layer, h_dtype):
    RT, D = cfg.RT, cfg.D
    x_specs, x_args = _split_rows(cfg, x)
    return pl.pallas_call(
        functools.partial(_modulate_kernel, cfg=cfg),
        grid=(cfg.T // RT,),
        in_specs=x_specs + [_mod_spec(cfg, layer)],
        out_specs=pl.BlockSpec((RT, D), lambda i: (i, 0)),
        out_shape=jax.ShapeDtypeStruct((cfg.T, D), h_dtype),
        compiler_params=_params(("parallel",), 32),
        name="modulate_first",
    )(*x_args, mod)


def _deepnorm_ln(x, f, gate, g, b, alpha):
    y = alpha * x + gate * f
    mu = jnp.mean(y, axis=-1, keepdims=True)
    yc = y - mu
    var = jnp.mean(yc * yc, axis=-1, keepdims=True)
    return yc * lax.rsqrt(var + LN_EPS) * g + b


def _ln_router_kernel(*refs, cfg):
    x_refs = refs[:-13]
    (f_ref, m_ref, g_ref, b_ref, whi_ref, wlo_ref, br_ref,
     xo_ref, ho_ref, ridx_ref, rprob_ref, cnt_ref, carry_ref) = refs[-13:]
    i = pl.program_id(0)
    RT, n_exp, alpha = cfg.RT, cfg.E, cfg.alpha

    @pl.when(i == 0)
    def _():
        carry_ref[...] = jnp.zeros_like(carry_ref)

    m = m_ref[...]
    xn = _deepnorm_ln(_load_rows(cfg, x_refs), f_ref[...], m[2:3], g_ref[...], b_ref[...], alpha)
    xo_ref[...] = xn
    h = xn * (1.0 + m[4:5]) + m[3:4]
    ho_ref[...] = h
    h_hi = h.astype(BF16)
    h_lo = (h - h_hi.astype(F32)).astype(BF16)
    w_hi = whi_ref[...]
    logits = (jnp.dot(h_hi, w_hi, preferred_element_type=F32)
              + jnp.dot(h_lo, w_hi, preferred_element_type=F32)
              + jnp.dot(h_hi, wlo_ref[...], preferred_element_type=F32)) + br_ref[...]
    lane = lax.broadcasted_iota(I32, (RT, LANES), 1).astype(F32)
    neg = jnp.float32(-jnp.inf)
    l = jnp.where(lane < n_exp, logits, neg)
    vals, idxs = [], []
    multi = jnp.zeros((RT, LANES), F32)
    for _ in range(TOP_K):
        mx = jnp.max(l, axis=-1, keepdims=True)
        ik = jnp.min(jnp.where(l == mx, lane, float(LANES)), axis=-1, keepdims=True)
        sel = lane == ik
        vals.append(mx)
        idxs.append(ik)
        multi = multi + sel.astype(F32)
        l = jnp.where(sel, neg, l)
    exps = [jnp.exp(v - vals[0]) for v in vals]
    den = exps[0]
    for e in exps[1:]:
        den = den + e
    r_i = lax.broadcasted_iota(I32, (RT, RT), 0)
    c_i = lax.broadcasted_iota(I32, (RT, RT), 1)
    tri = (c_i < r_i).astype(BF16)
    prefix = jnp.dot(tri, multi.astype(BF16), preferred_element_type=F32) + carry_ref[0:1, :]
    ridx = jnp.zeros((RT, LANES), F32)
    rprob = jnp.zeros((RT, LANES), F32)
    for k in range(TOP_K):
        rank = jnp.sum(jnp.where(lane == idxs[k], prefix, 0.0), axis=-1, keepdims=True)
        ridx = jnp.where(lane == k, idxs[k], ridx)
        ridx = jnp.where(lane == TOP_K + k, rank, ridx)
        rprob = jnp.where(lane == k, exps[k] / den, rprob)
    ridx_ref[...] = ridx.astype(I32)
    rprob_ref[...] = rprob
    carry = carry_ref[0:1, :] + jnp.sum(multi, axis=0, keepdims=True)
    carry_ref[...] = jnp.broadcast_to(carry, carry_ref.shape)
    cnt_ref[...] = jnp.broadcast_to(carry, cnt_ref.shape).astype(I32)


def ln_router(cfg, x, f, mod, layer, ln_g, ln_b, w_router, b_router):
    RT, D, T, E = cfg.RT, cfg.D, cfg.T, cfg.E
    wr = jnp.zeros((D, LANES), F32).at[:, :E].set(w_router[layer])
    w_hi = wr.astype(BF16)
    w_lo = (wr - w_hi.astype(F32)).astype(BF16)
    br = jnp.zeros((1, LANES), F32).at[0, :E].set(b_router[layer])
    row = lambda i: (i, 0)
    const = lambda i: (0, 0)
    x_specs, x_args = _split_rows(cfg, x)
    return pl.pallas_call(
        functools.partial(_ln_router_kernel, cfg=cfg),
        grid=(T // RT,),
        in_specs=x_specs + [pl.BlockSpec((RT, D), row), _mod_spec(cfg, layer),
                  pl.BlockSpec((None, None, 1, D), lambda i: (layer, 0, 0, 0)),
                  pl.BlockSpec((None, None, 1, D), lambda i: (layer, 0, 0, 0)),
                  pl.BlockSpec((D, LANES), const), pl.BlockSpec((D, LANES), const),
                  pl.BlockSpec((1, LANES), const)],
        out_specs=[pl.BlockSpec((RT, D), row), pl.BlockSpec((RT, D), row),
                   pl.BlockSpec((RT, LANES), row), pl.BlockSpec((RT, LANES), row),
                   pl.BlockSpec((SUBLANES, LANES), const)],
        out_shape=[jax.ShapeDtypeStruct((T, D), F32), jax.ShapeDtypeStruct((T, D), F32),
                   jax.ShapeDtypeStruct((T, LANES), I32), jax.ShapeDtypeStruct((T, LANES), F32),
                   jax.ShapeDtypeStruct((SUBLANES, LANES), I32)],
        scratch_shapes=[pltpu.VMEM((SUBLANES, LANES), F32)],
        compiler_params=_params(("arbitrary",), 48),
        name="ln_router",
    )(*x_args, f, mod, ln_g.reshape(cfg.depth, 2, 1, D), ln_b.reshape(cfg.depth, 2, 1, D), w_hi, w_lo, br)


def _ln_combine_kernel(pos_cur, pos_nxt, x_ref, p_ref, m_ref, g_ref, b_ref, *rest,
                       cfg, n_tiles, has_next):
    if has_next:
        mn_ref, y_hbm, xo_ref, ho_ref, buf, sem = rest
    else:
        y_hbm, xo_ref, xs_ref, buf, sem = rest
    i = pl.program_id(0)
    CT = x_ref.shape[0]
    rows = TOP_K * CT
    slot = i % 2

    def row_copy(src_row, dst_row, s):
        return pltpu.make_async_copy(y_hbm.at[pl.ds(src_row, 1)], buf.at[pl.ds(dst_row, 1)], sem.at[s])

    def wait(s):
        def body(j, c):
            row_copy(0, s * rows, s).wait()
            return c
        lax.fori_loop(0, rows, body, 0, unroll=8)

    @pl.when(i == 0)
    def _():
        def body(j, c):
            row_copy(pos_cur[j], j, 0).start()
            return c
        lax.fori_loop(0, rows, body, 0, unroll=8)

    wait(slot)
    p = p_ref[...]
    base = pl.multiple_of(slot * rows, rows)
    f = jnp.zeros(x_ref.shape, F32)
    for k in range(TOP_K):
        f = f + p[:, k:k + 1] * buf[pl.ds(base + k * CT, CT), :]
    nxt = (1 - slot) * rows
    for j in range(rows):
        row_copy(pos_nxt[j], nxt + j, 1 - slot).start()
    m = m_ref[...]
    xn = _deepnorm_ln(x_ref[...], f, m[5:6], g_ref[...], b_ref[...], cfg.alpha)
    if has_next:
        xo_ref[...] = xn
        mn = mn_ref[...]
        ho_ref[...] = (xn * (1.0 + mn[1:2]) + mn[0:1]).astype(ho_ref.dtype)
    else:
        n_first = cfg.T_p // CT

        @pl.when(i < n_first)
        def _():
            xo_ref[...] = xn

        @pl.when(i >= n_first)
        def _():
            xs_ref[...] = xn

    @pl.when(i == n_tiles - 1)
    def _():
        wait(1 - slot)


def ln_combine(cfg, x, y, pos_flat, rprob, mod, layer, ln_g, ln_b, next_dtype):
    CT, D, T = cfg.CT, cfg.D, cfg.T
    n_tiles = T // CT
    per = cfg.RT // CT
    rows = TOP_K * CT
    has_next = next_dtype is not None
    row = lambda i: (i, 0)
    mod_cur = pl.BlockSpec((None, None, N_MOD, D), lambda i: (layer, cfg.cond_row(i // per), 0, 0))
    in_specs = [pl.BlockSpec((rows,), lambda i: (i,), memory_space=pltpu.SMEM),
                pl.BlockSpec((rows,), lambda i: (jnp.minimum(i + 1, n_tiles - 1),), memory_space=pltpu.SMEM),
                pl.BlockSpec((CT, D), row), pl.BlockSpec((CT, LANES), row), mod_cur,
                pl.BlockSpec((None, None, 1, D), lambda i: (layer, 1, 0, 0)),
                pl.BlockSpec((None, None, 1, D), lambda i: (layer, 1, 0, 0))]
    args = [pos_flat, pos_flat, x, rprob, mod, ln_g.reshape(cfg.depth, 2, 1, D), ln_b.reshape(cfg.depth, 2, 1, D)]
    if has_next:
        in_specs.append(pl.BlockSpec((None, None, N_MOD, D),
                                     lambda i: (layer + 1, cfg.cond_row(i // per), 0, 0)))
        args.append(mod)
        out_specs = [pl.BlockSpec((CT, D), row), pl.BlockSpec((CT, D), row)]
        out_shape = [jax.ShapeDtypeStruct((T, D), F32), jax.ShapeDtypeStruct((T, D), next_dtype)]
    else:
        n_first = cfg.T_p // CT
        out_specs = [pl.BlockSpec((CT, D), lambda i: (jnp.minimum(i, n_first - 1), 0)),
                     pl.BlockSpec((CT, D), lambda i: (jnp.maximum(i - n_first, 0), 0))]
        out_shape = [jax.ShapeDtypeStruct((cfg.T_p, D), F32), jax.ShapeDtypeStruct((cfg.T_s, D), F32)]
    in_specs.append(pl.BlockSpec(memory_space=pl.ANY))
    args.append(y)
    out = pl.pallas_call(
        functools.partial(_ln_combine_kernel, cfg=cfg, n_tiles=n_tiles, has_next=has_next),
        grid=(n_tiles,),
        in_specs=in_specs, out_specs=out_specs, out_shape=out_shape,
        scratch_shapes=[pltpu.VMEM((2 * rows, D), F32), pltpu.SemaphoreType.DMA((2,))],
        compiler_params=_params(("arbitrary",), 48),
        name="ln_combine",
    )(*args)
    return out[0], out[1]


def _mm_kernel(*refs, has_scale, n_first):
    a_refs, refs = (refs[:1], refs[1:]) if n_first is None else (refs[:2], refs[2:])
    if has_scale:
        w_ref, b_ref, s_ref, o_ref, wbf = refs
    else:
        w_ref, b_ref, o_ref, wbf = refs
    i = pl.program_id(2)

    @pl.when(i == 0)
    def _():
        wbf[...] = w_ref[...].astype(BF16)

    def emit(a_ref):
        acc = jnp.dot(a_ref[...], wbf[...], preferred_element_type=F32) + b_ref[...]
        if has_scale:
            acc = acc * s_ref[...]
        o_ref[...] = acc.astype(o_ref.dtype)

    if n_first is None:
        emit(a_refs[0])
    else:
        pl.when(i < n_first)(lambda: emit(a_refs[0]))
        pl.when(i >= n_first)(lambda: emit(a_refs[1]))


def matmul(cfg, a, w4, layer, bias, scale=None, out_dtype=F32, name="matmul"):
    _, G, K, N = w4.shape
    tm, tn = cfg.mm_tile, min(cfg.col_tile, N)
    n_a = 2 if isinstance(a, tuple) else 1
    out_bytes = jnp.dtype(out_dtype).itemsize
    vmem_need = lambda m: (n_a * 2 * m * K * 2 + 2 * K * tn * 4 + K * tn * 2 + 2 * m * tn * out_bytes) / MIB
    while vmem_need(tm) > MM_VMEM_LIMIT_MIB - 4 and tm > SUBLANES:
        tm //= 2
    nj = N // tn
    col = lambda g, j, i: (0, g * nj + j)
    if isinstance(a, tuple):
        n_first = a[0].shape[0] // tm
        M = a[0].shape[0] + a[1].shape[0]
        in_specs = [pl.BlockSpec((tm, K), lambda g, j, i: (jnp.minimum(i, n_first - 1), g)),
                    pl.BlockSpec((tm, K), lambda g, j, i: (jnp.maximum(i - n_first, 0), g))]
        args = list(a)
    else:
        n_first = None
        M = a.shape[0]
        in_specs = [pl.BlockSpec((tm, K), lambda g, j, i: (i, g))]
        args = [a]
    in_specs += [pl.BlockSpec((None, None, K, tn), lambda g, j, i: (layer, g, 0, j)),
                 pl.BlockSpec((1, tn), col)]
    args += [w4, bias]
    if scale is not None:
        in_specs.append(pl.BlockSpec((1, tn), col))
        args.append(scale)
    return pl.pallas_call(
        functools.partial(_mm_kernel, has_scale=scale is not None, n_first=n_first),
        grid=(G, nj, M // tm),
        in_specs=in_specs,
        out_specs=pl.BlockSpec((tm, tn), lambda g, j, i: (i, g * nj + j)),
        out_shape=jax.ShapeDtypeStruct((M, G * N), out_dtype),
        scratch_shapes=[pltpu.VMEM((K, tn), BF16)],
        compiler_params=_params(("parallel", "parallel", "arbitrary"), MM_VMEM_LIMIT_MIB),
        name=name,
    )(*args)


def _hyconv_kernel(*refs, cfg):
    z_refs, w_refs, b_refs = refs[:9], refs[9:12], refs[12:15]
    u_ref, x0_ref = refs[15:17]
    i = pl.program_id(0)
    RT = cfg.RT
    first, last, _ = cfg.seq_edges(i)
    row = lax.broadcasted_iota(I32, u_ref.shape, 0)

    def conv(zc, zp, zn, w, b):
        z = zc[...]
        prev_row = jnp.where(first, 0.0, zp[SUBLANES - 1:SUBLANES, :])
        next_row = jnp.where(last, 0.0, zn[0:1, :])
        z_m = jnp.where(row == 0, prev_row, pltpu.roll(z, 1, 0))
        z_p = jnp.where(row == RT - 1, next_row, pltpu.roll(z, RT - 1, 0))
        wv = w[...]
        return wv[0:1] * z_m + wv[1:2] * z + wv[2:3] * z_p + b[...]

    x0, x1, v = [conv(*z_refs[3 * g:3 * g + 3], w_refs[g], b_refs[g]) for g in range(3)]
    u_ref[...] = v * x1
    x0_ref[...] = x0


def hyena_short_conv(cfg, z, conv_w, conv_b, j):
    RT, D, T, tc = cfg.RT, cfg.D, cfg.T, cfg.ew_tile
    nc = D // tc
    r8 = RT // SUBLANES
    in_specs, args = [], []
    for g in range(3):
        in_specs += [pl.BlockSpec((RT, tc), lambda i, c, g=g: (i, g * nc + c)),
                     pl.BlockSpec((SUBLANES, tc), lambda i, c, g=g: (jnp.maximum(i * r8 - 1, 0), g * nc + c)),
                     pl.BlockSpec((SUBLANES, tc),
                                  lambda i, c, g=g: (jnp.minimum((i + 1) * r8, T // SUBLANES - 1), g * nc + c))]
        args += [z, z, z]
    for g in range(3):
        in_specs.append(pl.BlockSpec((None, SHORT_CONV, tc), lambda i, c, g=g: (j, 0, g * nc + c)))
        args.append(conv_w)
    for g in range(3):
        in_specs.append(pl.BlockSpec((None, 1, tc), lambda i, c, g=g: (j, 0, g * nc + c)))
        args.append(conv_b.reshape(conv_b.shape[0], 1, 3 * D))
    blk = pl.BlockSpec((RT, tc), lambda i, c: (i, c))
    return pl.pallas_call(
        functools.partial(_hyconv_kernel, cfg=cfg),
        grid=(T // RT, nc),
        in_specs=in_specs, out_specs=[blk, blk],
        out_shape=[jax.ShapeDtypeStruct((T, D), F32), jax.ShapeDtypeStruct((T, D), F32)],
        compiler_params=_params(("parallel", "parallel"), 32),
        name="hyena_short_conv",
    )(*args)


def _filter_kernel(z_ref, t_ref, sg_ref, w1_ref, w23_ref, b_ref, fr_ref, wf_ref, wb_ref, dl_ref,
                   x1_ref, x2_ref, kl_ref, hdn_ref):
    @pl.when(pl.program_id(0) == 0)
    def _():
        b = b_ref[...]
        fr = fr_ref[...]
        hdn = jnp.sin(fr[0:1] * (jnp.dot(z_ref[...], w1_ref[...], precision=HIGHEST,
                                         preferred_element_type=F32) + b[0:1]))
        for s in range(2):
            hdn = jnp.sin(fr[s + 1:s + 2] * (jnp.dot(hdn, w23_ref[s], precision=HIGHEST,
                                                     preferred_element_type=F32) + b[s + 1:s + 2]))
        hdn_ref[...] = hdn

    hdn = hdn_ref[...]
    t = t_ref[...]
    win = jnp.exp(-t * dl_ref[...]) + HYENA_DECAY_SHIFT
    hf = jnp.dot(hdn, wf_ref[...], precision=HIGHEST, preferred_element_type=F32) * win
    hb = jnp.dot(hdn, wb_ref[...], precision=HIGHEST, preferred_element_type=F32) * win
    row = lax.broadcasted_iota(I32, hb.shape, 0)
    hb = jnp.where(row == 0, 0.0, hb)
    norm = jnp.sum(jnp.abs(hf), axis=0, keepdims=True) + jnp.sum(jnp.abs(hb), axis=0, keepdims=True)
    x1 = (hf + hb) / norm
    x2 = (hb - hf) / norm
    x1_ref[...] = x1.astype(x1_ref.dtype)
    x2_ref[...] = x2.astype(x2_ref.dtype)
    kl_ref[...] = jnp.sum(sg_ref[...] * x1, axis=0, keepdims=True)


def hyena_position_features(L):
    t = jnp.linspace(0.0, 1.0, L, dtype=F32)[:, None]
    bands = (HYENA_EMB_DIM - 1) // 2
    w = 2.0 * math.pi * jnp.arange(L, dtype=F32)[:, None] / L
    f = jnp.linspace(1e-4, bands - 1, bands, dtype=F32)[None, :]
    return t, jnp.concatenate([t, jnp.cos(f * w), -jnp.sin(f * w)], axis=-1)


def hyena_filter_parts(cfg, L, j, w1, w23, b, freq, w_out):
    D, tc = cfg.D, cfg.filt_tile
    width = w1.shape[-1]
    t, z = hyena_position_features(L)
    zp = jnp.zeros((L, LANES), F32).at[:, :HYENA_EMB_DIM].set(z)
    w1p = jnp.zeros((LANES, width), F32).at[:HYENA_EMB_DIM].set(w1[j])
    sign = (1 - 2 * (jnp.arange(L) % 2)).astype(F32)[:, None]
    min_decay = math.log(HYENA_DECAY_TARGET) / HYENA_SLOW_DECAY
    max_decay = math.log(HYENA_DECAY_TARGET) / HYENA_FAST_DECAY
    deltas = jnp.abs(jnp.linspace(min_decay, max_decay, D, dtype=F32))[None, :]
    nc = D // tc
    const2 = lambda c: (0, 0)
    return pl.pallas_call(
        _filter_kernel,
        grid=(nc,),
        in_specs=[pl.BlockSpec((L, LANES), const2), pl.BlockSpec((L, 1), const2), pl.BlockSpec((L, 1), const2),
                  pl.BlockSpec((LANES, width), const2),
                  pl.BlockSpec((None, 2, width, width), lambda c: (j, 0, 0, 0)),
                  pl.BlockSpec((None, 3, width), lambda c: (j, 0, 0)),
                  pl.BlockSpec((None, 3, width), lambda c: (j, 0, 0)),
                  pl.BlockSpec((None, width, tc), lambda c: (j, 0, c)),
                  pl.BlockSpec((None, width, tc), lambda c: (j, 0, nc + c)),
                  pl.BlockSpec((1, tc), lambda c: (0, c))],
        out_specs=[pl.BlockSpec((L, tc), lambda c: (0, c)), pl.BlockSpec((L, tc), lambda c: (0, c)),
                   pl.BlockSpec((1, tc), lambda c: (0, c))],
        out_shape=[jax.ShapeDtypeStruct((L, D), BF16), jax.ShapeDtypeStruct((L, D), BF16),
                   jax.ShapeDtypeStruct((1, D), F32)],
        scratch_shapes=[pltpu.VMEM((L, width), F32)],
        compiler_params=_params(("arbitrary",), 48),
        name="hyena_filter",
    )(zp, t, sign, w1p, w23, b, freq, w_out, w_out, deltas)


def dft_matrices(L):
    f = jnp.arange(L, dtype=I32)[:, None]
    t = jnp.arange(L, dtype=I32)[None, :]
    ang = ((f * t) % (2 * L)).astype(F32) * (math.pi / L)
    c, s = jnp.cos(ang), jnp.sin(ang)
    sign = (1 - 2 * (t % 2)).astype(F32)
    fwd = jnp.concatenate([c, jnp.where(f == 0, sign, s)], axis=0)
    ga = c * jnp.where(f == 0, 0.5, 1.0) / L
    gb = jnp.where(f == 0, sign / (2 * L), -s / L)
    inv = jnp.concatenate([ga.T, gb.T], axis=1)
    return fwd.astype(BF16), inv.astype(BF16)


def _kf_kernel(f_ref, x1_ref, x2_ref, kl_ref, o_ref, *, nk):
    k2 = pl.program_id(1)

    @pl.when(k2 < nk)
    def _():
        o_ref[...] = jnp.dot(f_ref[...], x1_ref[...], preferred_element_type=F32)

    @pl.when(k2 >= nk)
    def _():
        o_ref[...] = jnp.dot(f_ref[...], x2_ref[...], preferred_element_type=F32)

    @pl.when(k2 == nk)
    def _():
        o_ref[0:1, :] = kl_ref[...]


def hyena_filter_spectrum(cfg, L, fwd, x1, x2, kl):
    D, tc = cfg.D, cfg.col_tile
    tf = min(cfg.freq_tile, L)
    nk = L // tf
    return pl.pallas_call(
        functools.partial(_kf_kernel, nk=nk),
        grid=(D // tc, 2 * nk),
        in_specs=[pl.BlockSpec((tf, L), lambda c, k: (k, 0)),
                  pl.BlockSpec((L, tc), lambda c, k: (0, c)), pl.BlockSpec((L, tc), lambda c, k: (0, c)),
                  pl.BlockSpec((1, tc), lambda c, k: (0, c))],
        out_specs=pl.BlockSpec((tf, tc), lambda c, k: (k, c)),
        out_shape=jax.ShapeDtypeStruct((2 * L, D), F32),
        compiler_params=_params(("parallel", "arbitrary"), 32),
        name="hyena_filter_spectrum",
    )(fwd, x1, x2, kl)


def _lconv_kernel(u_ref, x0_ref, fb_ref, fa_ref, fs_ref, ga_ref, gs_ref, ka_ref, ks_ref, o_ref, acc,
                  *, L, n_seq, nk):
    k = pl.program_id(2)

    @pl.when(k == 0)
    def _():
        acc[...] = jnp.zeros_like(acc)

    ka, ks = ka_ref[...], ks_ref[...]
    row0 = (lax.broadcasted_iota(I32, ka.shape, 0) == 0) & (k == 0)
    for s in range(n_seq):
        rows = pl.ds(s * L, L)
        us = u_ref[rows, :].astype(BF16)
        a = jnp.dot(fa_ref[...], us, preferred_element_type=F32)
        b = jnp.dot(fs_ref[...], us, preferred_element_type=F32)
        bk = b * ks
        ya = a * ka + jnp.where(row0, 0.0, bk)
        ys = jnp.where(row0, bk, a * ks - b * ka)
        acc[rows, :] += (jnp.dot(ga_ref[...], ya.astype(BF16), preferred_element_type=F32)
                         + jnp.dot(gs_ref[...], ys.astype(BF16), preferred_element_type=F32))

    @pl.when(k == nk - 1)
    def _():
        o_ref[...] = ((acc[...] + u_ref[...] * fb_ref[...]) * x0_ref[...]).astype(o_ref.dtype)


def hyena_long_conv(cfg, L, row_off, n_rows, u, x0, filter_bias, j, fwd, inv, kf):
    D, tc = cfg.D, cfg.col_tile
    RB = max(L, cfg.conv_rows)
    n_seq = RB // L
    tf = min(cfg.freq_tile, L)
    nk = L // tf
    off = row_off // RB
    blk = pl.BlockSpec((RB, tc), lambda r, c, k: (off + r, c))
    in_specs = [blk, blk,
                pl.BlockSpec((None, 1, tc), lambda r, c, k: (j, 0, c)),
                pl.BlockSpec((tf, L), lambda r, c, k: (k, 0)), pl.BlockSpec((tf, L), lambda r, c, k: (nk + k, 0)),
                pl.BlockSpec((L, tf), lambda r, c, k: (0, k)), pl.BlockSpec((L, tf), lambda r, c, k: (0, nk + k)),
                pl.BlockSpec((tf, tc), lambda r, c, k: (k, c)), pl.BlockSpec((tf, tc), lambda r, c, k: (nk + k, c))]
    args = [u, x0, filter_bias.reshape(filter_bias.shape[0], 1, D), fwd, fwd, inv, inv, kf, kf]
    return pl.pallas_call(
        functools.partial(_lconv_kernel, L=L, n_seq=n_seq, nk=nk),
        grid=(n_rows // RB, D // tc, nk),
        in_specs=in_specs, out_specs=pl.BlockSpec((RB, tc), lambda r, c, k: (r, c)),
        out_shape=jax.ShapeDtypeStruct((n_rows, D), BF16),
        scratch_shapes=[pltpu.VMEM((RB, tc), F32)],
        compiler_params=_params(("parallel", "parallel", "arbitrary"), 56),
        name="hyena_long_conv",
    )(*args)


def hyena_mixer(cfg, h, j, hy_w_in, hy_b_in, hy_conv_w, hy_conv_b, filt, hy_filter_bias, hy_w_out, hy_b_out):
    D = cfg.D
    z = matmul(cfg, h, hy_w_in.reshape(hy_w_in.shape[0], 1, D, 3 * D), j, hy_b_in[j][None, :], name="hyena_in")
    u, x0 = hyena_short_conv(cfg, z, hy_conv_w, hy_conv_b, j)
    out = []
    for L, row_off, n_rows in ((cfg.seq, 0, cfg.T_p), (cfg.dec_seq, cfg.T_p, cfg.T_s)):
        x1, x2, kl = hyena_filter_parts(cfg, L, j, *filt)
        fwd, inv = dft_matrices(L)
        kf = hyena_filter_spectrum(cfg, L, fwd, x1, x2, kl)
        out.append(hyena_long_conv(cfg, L, row_off, n_rows, u, x0, hy_filter_bias, j, fwd, inv, kf))
    return matmul(cfg, tuple(out), hy_w_out.reshape(hy_w_out.shape[0], 1, D, D), j, hy_b_out[j][None, :], name="hyena_out")


def _qk_post_kernel(x_ref, g_ref, cos_ref, sin_ref, o_ref, *rest, cfg, emit_norm):
    i = pl.program_id(0)
    hd = cfg.hd
    g = g_ref[...]
    lane = lax.broadcasted_iota(I32, (x_ref.shape[0], hd), 1)
    low = (lane % (hd // 2)) < (hd // 4)
    is_latent = i >= cfg.NP_T
    for hh in range(x_ref.shape[1] // hd):
        cols = slice(hh * hd, (hh + 1) * hd)
        x = x_ref[:, cols]
        xn = x * lax.rsqrt(jnp.mean(x * x, axis=-1, keepdims=True) + RMS_EPS) * g
        if emit_norm:
            rest[0][:, cols] = xn
        swapped = jnp.where(low, pltpu.roll(xn, hd - hd // 4, 1), pltpu.roll(xn, hd // 4, 1))
        roped = xn * cos_ref[...] + swapped * sin_ref[...]
        o_ref[:, cols] = jnp.where(is_latent, roped, xn).astype(o_ref.dtype)


def axial_rope_tables(cfg):
    L, hd = cfg.dec_seq, cfg.hd
    axis_dim = hd // 2
    rows = L // cfg.grid_w
    row = jnp.repeat(jnp.arange(rows), cfg.grid_w)
    col = jnp.tile(jnp.arange(cfg.grid_w), rows)
    inv_freq = ROPE_THETA ** (-jnp.arange(0, axis_dim, 2, dtype=F32) / axis_dim)
    ang = jnp.stack([row, col], axis=-1).astype(F32)[..., None] * inv_freq
    cos, sin = jnp.cos(ang), jnp.sin(ang)
    cos_t = jnp.concatenate([cos, cos], axis=-1).reshape(L, hd)
    sin_t = jnp.concatenate([-sin, sin], axis=-1).reshape(L, hd)
    return cos_t, sin_t


def qk_post(cfg, qkv, norm_w, j, cos_t, sin_t, col_off, n_cols, emit_norm):
    RT, T, hd = cfg.RT, cfg.T, cfg.hd
    tc = min(cfg.ew_tile, n_cols)
    blk_off = col_off // tc
    pos_blk = lambda i, c: (jnp.where(i < cfg.NP_T, 0, (i - cfg.NP_T) % cfg.TPS), 0)
    out_specs = [pl.BlockSpec((RT, tc), lambda i, c: (i, c))]
    out_shape = [jax.ShapeDtypeStruct((T, n_cols), BF16)]
    if emit_norm:
        out_specs.append(pl.BlockSpec((RT, tc), lambda i, c: (i, c)))
        out_shape.append(jax.ShapeDtypeStruct((T, n_cols), F32))
    return pl.pallas_call(
        functools.partial(_qk_post_kernel, cfg=cfg, emit_norm=emit_norm),
        grid=(T // RT, n_cols // tc),
        in_specs=[pl.BlockSpec((RT, tc), lambda i, c: (i, blk_off + c)),
                  pl.BlockSpec((None, 1, hd), lambda i, c: (j, 0, 0)),
                  pl.BlockSpec((RT, hd), pos_blk), pl.BlockSpec((RT, hd), pos_blk)],
        out_specs=out_specs, out_shape=out_shape,
        compiler_params=_params(("parallel", "parallel"), 32),
        name="qk_norm_rope",
    )(qkv, norm_w.reshape(norm_w.shape[0], 1, hd), cos_t, sin_t)


def _attn_kernel(*refs, cfg, has_cache):
    if has_cache:
        q_ref, k_ref, v_ref, kc_ref, vc_ref, o_ref = refs
    else:
        q_ref, k_ref, v_ref, o_ref = refs
    hd = cfg.hd
    scale = hd ** -0.5
    nt = (((1,), (1,)), ((), ()))
    k = k_ref[...]
    v = v_ref[...].astype(BF16)
    if has_cache:
        kc = kc_ref[...].astype(BF16)
        vc = vc_ref[...].astype(BF16)
    for g in range(cfg.gqa):
        cols = slice(g * hd, (g + 1) * hd)
        q = q_ref[:, cols]
        s = lax.dot_general(q, k, nt, preferred_element_type=F32) * scale
        m = jnp.max(s, axis=-1, keepdims=True)
        if has_cache:
            sc = lax.dot_general(q, kc, nt, preferred_element_type=F32) * scale
            m = jnp.maximum(m, jnp.max(sc, axis=-1, keepdims=True))
        p = jnp.exp(s - m)
        l = jnp.sum(p, axis=-1, keepdims=True)
        o = jnp.dot(p.astype(BF16), v, preferred_element_type=F32)
        if has_cache:
            pc = jnp.exp(sc - m)
            l = l + jnp.sum(pc, axis=-1, keepdims=True)
            o = o + jnp.dot(pc.astype(BF16), vc, preferred_element_type=F32)
        o_ref[:, cols] = (o / l).astype(o_ref.dtype)


def attention(cfg, q, k, qkv, n_seq, L, row_off, cache):
    RT, D, hd, G = cfg.RT, cfg.D, cfg.hd, cfg.gqa
    v_col = cfg.n_heads + cfg.n_kv
    q_off, kv_off, nq = row_off // RT, row_off // L, L // RT
    in_specs = [pl.BlockSpec((RT, G * hd), lambda b, h, t: (q_off + b * nq + t, h)),
                pl.BlockSpec((L, hd), lambda b, h, t: (kv_off + b, h)),
                pl.BlockSpec((L, hd), lambda b, h, t: (kv_off + b, v_col + h))]
    args = [q, k, qkv]
    if cache is not None:
        ck, cv, n_attn, j = cache
        past = ck.shape[0] // (n_seq * n_attn)
        spec = pl.BlockSpec((past, hd), lambda b, h, t: (b * n_attn + j, h))
        in_specs += [spec, spec]
        args += [ck, cv]
    return pl.pallas_call(
        functools.partial(_attn_kernel, cfg=cfg, has_cache=cache is not None),
        grid=(n_seq, cfg.n_kv, nq),
        in_specs=in_specs,
        out_specs=pl.BlockSpec((RT, G * hd), lambda b, h, t: (b * nq + t, h)),
        out_shape=jax.ShapeDtypeStruct((n_seq * L, D), BF16),
        compiler_params=_params(("parallel", "parallel", "parallel"), 48),
        name="attention",
    )(*args)


def attention_mixer(cfg, h, j, cache_k, cache_v, at_w_qkv, at_q_norm, at_k_norm, at_w_o):
    D, hd = cfg.D, cfg.hd
    n_attn = at_w_qkv.shape[0]
    qkv_dim = at_w_qkv.shape[-1]
    kv_dim = cfg.n_kv * hd
    qkv = matmul(cfg, h, at_w_qkv.reshape(n_attn, 1, D, qkv_dim), j, jnp.zeros((1, qkv_dim), F32), name="attn_qkv")
    cos_t, sin_t = axial_rope_tables(cfg)
    (q,) = qk_post(cfg, qkv, at_q_norm, j, cos_t, sin_t, 0, D, False)
    k, k_norm = qk_post(cfg, qkv, at_k_norm, j, cos_t, sin_t, D, kv_dim, True)
    o_p = attention(cfg, q, k, qkv, cfg.batch, cfg.seq, 0, None)
    ck = cache_k.reshape(-1, kv_dim)
    cv = cache_v.reshape(-1, kv_dim)
    o_s = attention(cfg, q, k, qkv, cfg.dec_batch, cfg.dec_seq, cfg.T_p, (ck, cv, n_attn, j))
    f = matmul(cfg, (o_p, o_s), at_w_o.reshape(n_attn, 1, D, D), j, jnp.zeros((1, D), F32), name="attn_out")
    new_k = k_norm[:cfg.T_p].reshape(cfg.batch, cfg.seq, cfg.n_kv, hd)
    new_v = qkv[:cfg.T_p, D + kv_dim:].reshape(cfg.batch, cfg.seq, cfg.n_kv, hd)
    return f, new_k, new_v


def _pool_kernel(hc, hp, hn, o_ref, ext, *, cfg, cols_per_group):
    i = pl.program_id(0)
    c = pl.program_id(1)
    RT = cfg.RT
    H = SUBLANES
    first, last, p0 = cfg.seq_edges(i)
    L = jnp.where(i < cfg.NP_T, cfg.seq, cfg.dec_seq)
    ext[0:H, :] = jnp.where(first, 0.0, hp[...])
    ext[H:H + RT, :] = hc[...]
    ext[H + RT:, :] = jnp.where(last, 0.0, hn[...])
    pos = p0 + lax.broadcasted_iota(I32, (RT, 1), 0)
    for gi, w in enumerate(POOL_WINDOWS):
        @pl.when(c // cols_per_group == gi)
        def _(w=w):
            half = w // 2
            s = ext[H - half:H - half + RT, :]
            for o in range(1 - half, half):
                s = s + ext[H + o:H + o + RT, :]
            lo = jnp.clip(pos - half, 0, L)
            hi = jnp.clip(pos - half + w, 0, L)
            mean = s / (hi - lo).astype(F32)
            o_ref[...] = (mean - hc[...]).astype(o_ref.dtype)


def pool_mixer(cfg, h, j, pl_w, pl_b, pl_scale):
    RT, D, T = cfg.RT, cfg.D, cfg.T
    gd = D // cfg.n_pool
    tc = min(cfg.ew_tile, gd)
    r8 = RT // SUBLANES
    assert max(POOL_WINDOWS) // 2 <= SUBLANES
    d = pl.pallas_call(
        functools.partial(_pool_kernel, cfg=cfg, cols_per_group=gd // tc),
        grid=(T // RT, D // tc),
        in_specs=[pl.BlockSpec((RT, tc), lambda i, c: (i, c)),
                  pl.BlockSpec((SUBLANES, tc), lambda i, c: (jnp.maximum(i * r8 - 1, 0), c)),
                  pl.BlockSpec((SUBLANES, tc), lambda i, c: (jnp.minimum((i + 1) * r8, T // SUBLANES - 1), c))],
        out_specs=pl.BlockSpec((RT, tc), lambda i, c: (i, c)),
        out_shape=jax.ShapeDtypeStruct((T, D), BF16),
        scratch_shapes=[pltpu.VMEM((RT + 2 * SUBLANES, tc), F32)],
        compiler_params=_params(("parallel", "parallel"), 32),
        name="pool_window",
    )(h, h, h)
    return matmul(cfg, d, pl_w, j, pl_b[j][None, :], scale=pl_scale[j][None, :], name="pool_proj")


def _cast_kernel(x_ref, o_ref):
    o_ref[...] = x_ref[...].astype(o_ref.dtype)


def cast_bf16(w, rows=1024):
    shape = w.shape
    w2 = w.reshape(-1, shape[-1])
    rows = min(rows, w2.shape[0])
    out = pl.pallas_call(
        _cast_kernel,
        grid=(w2.shape[0] // rows,),
        in_specs=[pl.BlockSpec((rows, shape[-1]), lambda i: (i, 0))],
        out_specs=pl.BlockSpec((rows, shape[-1]), lambda i: (i, 0)),
        out_shape=jax.ShapeDtypeStruct(w2.shape, BF16),
        compiler_params=_params(("parallel",), 48),
        name="cast_bf16",
    )(w2)
    return out.reshape(shape)


def _moe_up_kernel(te_ref, nu_ref, src_cur, src_nxt, h_hbm, w_ref, b_ref, o_ref, buf, sem, *, F):
    t = pl.program_id(0)
    n_used = nu_ref[0]
    TM = o_ref.shape[0]
    slot = t % 2

    def row_copy(src_row, dst_row, s):
        return pltpu.make_async_copy(h_hbm.at[pl.ds(src_row, 1)], buf.at[pl.ds(dst_row, 1)], sem.at[s])

    def issue(src_ref, s):
        def body(r, c):
            row_copy(src_ref[r], s * TM + r, s).start()
            return c
        lax.fori_loop(0, TM, body, 0, unroll=8)

    def wait(s):
        def body(r, c):
            row_copy(0, s * TM, s).wait()
            return c
        lax.fori_loop(0, TM, body, 0, unroll=8)

    @pl.when((t == 0) & (n_used > 0))
    def _():
        issue(src_cur, 0)

    @pl.when(t + 1 < n_used)
    def _():
        issue(src_nxt, 1 - slot)

    @pl.when(t < n_used)
    def _():
        wait(slot)
        x = buf[pl.ds(pl.multiple_of(slot * TM, TM), TM), :].astype(BF16)
        gu = jnp.dot(x, w_ref[...], preferred_element_type=F32) + b_ref[...]
        g = jnp.minimum(gu[:, :F], SWIGLU_LIMIT)
        u = jnp.clip(gu[:, F:], -SWIGLU_LIMIT, SWIGLU_LIMIT)
        o_ref[...] = (g * jax.nn.sigmoid(SWIGLU_ALPHA * g) * (u + 1.0)).astype(o_ref.dtype)

    @pl.when(t >= n_used)
    def _():
        o_ref[...] = jnp.zeros_like(o_ref)


def moe_up(cfg, h, src, tile_expert, n_used, w_gu_bf, b_gu, layer):
    TM, D, F, NT = cfg.TM, cfg.D, cfg.F, cfg.NT
    grid_spec = pltpu.PrefetchScalarGridSpec(
        num_scalar_prefetch=2,
        grid=(NT,),
        in_specs=[pl.BlockSpec((TM,), lambda t, te, nu: (t,), memory_space=pltpu.SMEM),
                  pl.BlockSpec((TM,), lambda t, te, nu: (jnp.minimum(t + 1, NT - 1),), memory_space=pltpu.SMEM),
                  pl.BlockSpec(memory_space=pl.ANY),
                  pl.BlockSpec((None, None, D, 2 * F), lambda t, te, nu: (layer, te[t], 0, 0)),
                  pl.BlockSpec((None, None, 1, 2 * F), lambda t, te, nu: (layer, te[t], 0, 0))],
        out_specs=pl.BlockSpec((TM, F), lambda t, te, nu: (t, 0)),
        scratch_shapes=[pltpu.VMEM((2 * TM, D), F32), pltpu.SemaphoreType.DMA((2,))])
    return pl.pallas_call(
        functools.partial(_moe_up_kernel, F=F),
        grid_spec=grid_spec,
        out_shape=jax.ShapeDtypeStruct((cfg.P, F), BF16),
        compiler_params=_params(("arbitrary",), 56),
        name="moe_up",
    )(tile_expert, n_used, src, src, h, w_gu_bf, b_gu.reshape(cfg.depth, cfg.E, 1, 2 * F))


def _moe_down_kernel(te_ref, nu_ref, a_ref, w_ref, b_ref, o_ref, wbf):
    t = pl.program_id(0)

    @pl.when((t == 0) | (te_ref[t] != te_ref[jnp.maximum(t - 1, 0)]))
    def _():
        wbf[...] = w_ref[...].astype(BF16)

    @pl.when(t < nu_ref[0])
    def _():
        o_ref[...] = jnp.dot(a_ref[...], wbf[...], preferred_element_type=F32) + b_ref[...]

    @pl.when(t >= nu_ref[0])
    def _():
        o_ref[...] = jnp.zeros_like(o_ref)


def moe_down(cfg, a, tile_expert, n_used, w_down, b_down, layer):
    TM, D, F, NT = cfg.TM, cfg.D, cfg.F, cfg.NT
    grid_spec = pltpu.PrefetchScalarGridSpec(
        num_scalar_prefetch=2,
        grid=(NT,),
        in_specs=[pl.BlockSpec((TM, F), lambda t, te, nu: (t, 0)),
                  pl.BlockSpec((None, None, F, D), lambda t, te, nu: (layer, te[t], 0, 0)),
                  pl.BlockSpec((None, None, 1, D), lambda t, te, nu: (layer, te[t], 0, 0))],
        out_specs=pl.BlockSpec((TM, D), lambda t, te, nu: (t, 0)),
        scratch_shapes=[pltpu.VMEM((F, D), BF16)])
    return pl.pallas_call(
        _moe_down_kernel,
        grid_spec=grid_spec,
        out_shape=jax.ShapeDtypeStruct((cfg.P, D), F32),
        compiler_params=_params(("arbitrary",), 56),
        name="moe_down",
    )(tile_expert, n_used, a, w_down, b_down.reshape(cfg.depth, cfg.E, 1, D))


def route_metadata(cfg, ridx, cnt):
    T, E, TM, NT, CT = cfg.T, cfg.E, cfg.TM, cfg.NT, cfg.CT
    idx = ridx[:, :TOP_K]
    rank = ridx[:, TOP_K:2 * TOP_K]
    counts = cnt[0, :E]
    padded = ((counts + TM - 1) // TM) * TM
    e_i = jnp.arange(E, dtype=I32)
    ends = jnp.sum(jnp.where(e_i[None, :] <= e_i[:, None], padded[None, :], 0), axis=1)
    starts = ends - padded
    pos = rank + jnp.sum(jnp.where(idx[:, :, None] == e_i, starts, 0), axis=-1)
    token = jnp.broadcast_to(jnp.arange(T, dtype=I32)[:, None], (T, TOP_K))
    src = jnp.zeros((cfg.P,), I32).at[pos.reshape(-1)].set(token.reshape(-1), unique_indices=True)
    n_used = ends[-1] // TM
    tile_start = jnp.minimum(jnp.arange(NT, dtype=I32), n_used - 1) * TM
    te = jnp.sum((ends[None, :] <= tile_start[:, None]).astype(I32), axis=1)
    te = jnp.minimum(te, E - 1)
    pos_flat = pos.reshape(T // CT, CT, TOP_K).transpose(0, 2, 1).reshape(-1)
    return src, te, n_used.reshape(1), pos_flat


def kernel(x_prompt, x_sample, cache_k, cache_v, c, c_ctx, mod_w, mod_b, ln_g, ln_b, hy_w_in, hy_b_in, hy_conv_w, hy_conv_b, hy_ffn_w1, hy_ffn_w23, hy_ffn_b, hy_sin_freq, hy_ffn_w_out, hy_filter_bias, hy_w_out, hy_b_out, at_w_qkv, at_q_norm, at_k_norm, at_w_o, pl_w, pl_b, pl_scale, moe_w_router, moe_b_router, moe_w_gu, moe_b_gu, moe_w_down, moe_b_down):
    batch, seq, D = x_prompt.shape
    dec_batch, dec_seq, _ = x_sample.shape
    depth = mod_w.shape[0]
    cfg = Cfg(D, batch, seq, dec_batch, dec_seq, cache_k.shape[2], depth,
              moe_w_router.shape[-1], moe_w_down.shape[2])
    return trunk_step(cfg, x_prompt, x_sample, cache_k, cache_v, c, c_ctx, mod_w, mod_b, ln_g, ln_b,
                      hy_w_in, hy_b_in, hy_conv_w, hy_conv_b, hy_ffn_w1, hy_ffn_w23, hy_ffn_b, hy_sin_freq,
                      hy_ffn_w_out, hy_filter_bias, hy_w_out, hy_b_out, at_w_qkv, at_q_norm, at_k_norm, at_w_o,
                      pl_w, pl_b, pl_scale, moe_w_router, moe_b_router, moe_w_gu, moe_b_gu, moe_w_down,
                      moe_b_down)


def trunk_step(cfg, x_prompt, x_sample, cache_k, cache_v, c, c_ctx, mod_w, mod_b, ln_g, ln_b,
               hy_w_in, hy_b_in, hy_conv_w, hy_conv_b, hy_ffn_w1, hy_ffn_w23, hy_ffn_b, hy_sin_freq,
               hy_ffn_w_out, hy_filter_bias, hy_w_out, hy_b_out, at_w_qkv, at_q_norm, at_k_norm, at_w_o,
               pl_w, pl_b, pl_scale, moe_w_router, moe_b_router, moe_w_gu, moe_b_gu, moe_w_down, moe_b_down):
    D, depth = cfg.D, cfg.depth
    n_mixers = 3
    assert cfg.dec_batch + 1 <= COND_ROWS
    cond = jnp.zeros((COND_ROWS, D), F32).at[0].set(c_ctx).at[1:1 + cfg.dec_batch].set(c)
    mod = modulation_all(cfg, cond, mod_w, mod_b)
    w_gu_bf = cast_bf16(moe_w_gu)
    x = (x_prompt.reshape(cfg.T_p, D), x_sample.reshape(cfg.T_s, D))
    mixer_in_dtype = lambda i: F32 if i % n_mixers == 2 else BF16
    h = modulate_first(cfg, x, mod, 0, mixer_in_dtype(0))
    ctx_k, ctx_v = [], []
    for i in range(depth):
        kind, j = i % n_mixers, i // n_mixers
        if kind == 0:
            filt = (hy_ffn_w1, hy_ffn_w23, hy_ffn_b, hy_sin_freq, hy_ffn_w_out)
            f = hyena_mixer(cfg, h, j, hy_w_in, hy_b_in, hy_conv_w, hy_conv_b, filt,
                            hy_filter_bias, hy_w_out, hy_b_out)
        elif kind == 1:
            f, new_k, new_v = attention_mixer(cfg, h, j, cache_k, cache_v, at_w_qkv, at_q_norm, at_k_norm, at_w_o)
            ctx_k.append(new_k)
            ctx_v.append(new_v)
        else:
            f = pool_mixer(cfg, h, j, pl_w, pl_b, pl_scale)
        x, h_ffn, ridx, rprob, cnt = ln_router(cfg, x, f, mod, i, ln_g, ln_b, moe_w_router, moe_b_router)
        src, tile_expert, n_used, pos_flat = route_metadata(cfg, ridx, cnt)
        a = moe_up(cfg, h_ffn, src, tile_expert, n_used, w_gu_bf, moe_b_gu, i)
        y = moe_down(cfg, a, tile_expert, n_used, moe_w_down, moe_b_down, i)
        if i + 1 < depth:
            x, h = ln_combine(cfg, x, y, pos_flat, rprob, mod, i, ln_g, ln_b, mixer_in_dtype(i + 1))
        else:
            xp, xs = ln_combine(cfg, x, y, pos_flat, rprob, mod, i, ln_g, ln_b, None)
    xp = xp.reshape(cfg.batch, cfg.seq, D)
    xs = xs.reshape(cfg.dec_batch, cfg.dec_seq, D)
    return (xp, xs, jnp.stack(ctx_k, axis=1), jnp.stack(ctx_v, axis=1))
```

```python
import functools
import math

import jax
import jax.numpy as jnp
from jax import lax
from jax.experimental import pallas as pl
from jax.experimental.pallas import tpu as pltpu

F32 = jnp.float32
BF16 = jnp.bfloat16
I32 = jnp.int32

LANES = 128
SUBLANES = 8
VMEM_BYTES_V7X = 64 * 1024 * 1024
MIB = 1024 * 1024
MM_VMEM_LIMIT_MIB = 48

GRID_W = 64
N_HEADS = 32
N_KV_HEADS = 8
ROPE_THETA = 10000.0
RMS_EPS = 1e-6
SHORT_CONV = 3
HYENA_EMB_DIM = 33
HYENA_FAST_DECAY = 0.3
HYENA_SLOW_DECAY = 1.5
HYENA_DECAY_TARGET = 1e-2
HYENA_DECAY_SHIFT = 0.05
POOL_WINDOWS = (2, 4, 8, 16)
TOP_K = 4
SWIGLU_LIMIT = 7.0
SWIGLU_ALPHA = 1.702
LN_EPS = 1e-5
N_MOD = 6
COND_ROWS = 8
HIGHEST = lax.Precision.HIGHEST


class Cfg:
    def __init__(self, D, batch, seq, dec_batch, dec_seq, past, depth, n_exp, d_exp,
                 n_heads=N_HEADS, n_kv=N_KV_HEADS, grid_w=GRID_W,
                 mm_tile=1024, col_tile=512, ew_tile=2048, moe_tile=256, comb_tile=128, conv_rows=2048,
                 freq_tile=512, filt_tile=256):
        self.D, self.batch, self.seq = D, batch, seq
        self.dec_batch, self.dec_seq, self.past = dec_batch, dec_seq, past
        self.depth, self.E, self.F = depth, n_exp, d_exp
        self.n_heads, self.n_kv, self.grid_w = n_heads, n_kv, grid_w
        self.hd = D // n_heads
        self.gqa = n_heads // n_kv
        self.T_p = batch * seq
        self.T_s = dec_batch * dec_seq
        self.T = self.T_p + self.T_s
        self.RT = seq
        assert dec_seq % self.RT == 0 and self.RT % SUBLANES == 0
        self.NP_T = self.T_p // self.RT
        self.TPS = dec_seq // self.RT
        self.mm_tile = min(mm_tile, self.T_p)
        self.col_tile = min(col_tile, D)
        self.ew_tile = min(ew_tile, D)
        self.TM = moe_tile
        self.P = self.T * TOP_K + n_exp * self.TM
        self.NT = self.P // self.TM
        self.CT = min(comb_tile, self.RT)
        self.conv_rows = min(conv_rows, self.T_p)
        assert self.T_p % self.conv_rows == 0 and self.conv_rows % seq == 0
        assert dec_seq % self.conv_rows == 0 or self.conv_rows % dec_seq == 0
        self.freq_tile = freq_tile
        self.filt_tile = min(filt_tile, D)
        self.alpha = (2 * depth) ** 0.25
        self.n_pool = len(POOL_WINDOWS)

    def cond_row(self, i):
        return jnp.where(i < self.NP_T, 0, 1 + (i - self.NP_T) // self.TPS)

    def seq_edges(self, i):
        is_p = i < self.NP_T
        k = (i - self.NP_T) % self.TPS
        return is_p | (k == 0), is_p | (k == self.TPS - 1), jnp.where(is_p, 0, k * self.RT)


def _params(sem, vmem_mib):
    return pltpu.CompilerParams(dimension_semantics=sem, vmem_limit_bytes=int(vmem_mib * MIB))


def _silu(x):
    return x * jax.nn.sigmoid(x)


def _mod_kernel(c_ref, w_ref, b_ref, o_ref):
    s = _silu(c_ref[...]).astype(BF16)
    o_ref[...] = jnp.dot(s, w_ref[...].astype(BF16), preferred_element_type=F32) + b_ref[...]


def modulation_all(cfg, cond, mod_w, mod_b):
    depth, D, N = mod_w.shape
    tn = cfg.col_tile
    out = pl.pallas_call(
        _mod_kernel,
        grid=(depth, N // tn),
        in_specs=[pl.BlockSpec((COND_ROWS, D), lambda l, j: (0, 0)),
                  pl.BlockSpec((None, D, tn), lambda l, j: (l, 0, j)),
                  pl.BlockSpec((None, 1, tn), lambda l, j: (l, 0, j))],
        out_specs=pl.BlockSpec((None, COND_ROWS, tn), lambda l, j: (l, 0, j)),
        out_shape=jax.ShapeDtypeStruct((depth, COND_ROWS, N), F32),
        compiler_params=_params(("parallel", "parallel"), 40),
        name="modulation",
    )(cond, mod_w, mod_b.reshape(depth, 1, N))
    return out.reshape(depth, COND_ROWS, N_MOD, D)


def _mod_spec(cfg, layer):
    return pl.BlockSpec((None, None, N_MOD, cfg.D), lambda i, *_: (layer, cfg.cond_row(i), 0, 0))


def _split_rows(cfg, x):
    RT, D = cfg.RT, cfg.D
    if isinstance(x, tuple):
        return ([pl.BlockSpec((RT, D), lambda i: (jnp.minimum(i, cfg.NP_T - 1), 0)),
                 pl.BlockSpec((RT, D), lambda i: (jnp.maximum(i - cfg.NP_T, 0), 0))], list(x))
    return [pl.BlockSpec((RT, D), lambda i: (i, 0))], [x]


def _load_rows(cfg, x_refs):
    if len(x_refs) == 1:
        return x_refs[0][...]
    return jnp.where(pl.program_id(0) < cfg.NP_T, x_refs[0][...], x_refs[1][...])


def _modulate_kernel(*refs, cfg):
    x_refs, (m_ref, h_ref) = refs[:-2], refs[-2:]
    m = m_ref[...]
    h_ref[...] = (_load_rows(cfg, x_refs) * (1.0 + m[1:2]) + m[0:1]).astype(h_ref.dtype)


def modulate_first(cfg, x, mod, layer, h_dtype):
    RT, D = cfg.RT, cfg.D
    x_specs, x_args = _split_rows(cfg, x)
    return pl.pallas_call(
        functools.partial(_modulate_kernel, cfg=cfg),
        grid=(cfg.T // RT,),
        in_specs=x_specs + [_mod_spec(cfg, layer)],
        out_specs=pl.BlockSpec((RT, D), lambda i: (i, 0)),
        out_shape=jax.ShapeDtypeStruct((cfg.T, D), h_dtype),
        compiler_params=_params(("parallel",), 32),
        name="modulate_first",
    )(*x_args, mod)


def _deepnorm_ln(x, f, gate, g, b, alpha):
    y = alpha * x + gate * f
    mu = jnp.mean(y, axis=-1, keepdims=True)
    yc = y - mu
    var = jnp.mean(yc * yc, axis=-1, keepdims=True)
    return yc * lax.rsqrt(var + LN_EPS) * g + b


def _ln_router_kernel(*refs, cfg):
    x_refs = refs[:-13]
    (f_ref, m_ref, g_ref, b_ref, whi_ref, wlo_ref, br_ref,
     xo_ref, ho_ref, ridx_ref, rprob_ref, cnt_ref, carry_ref) = refs[-13:]
    i = pl.program_id(0)
    RT, n_exp, alpha = cfg.RT, cfg.E, cfg.alpha

    @pl.when(i == 0)
    def _():
        carry_ref[...] = jnp.zeros_like(carry_ref)

    m = m_ref[...]
    xn = _deepnorm_ln(_load_rows(cfg, x_refs), f_ref[...], m[2:3], g_ref[...], b_ref[...], alpha)
    xo_ref[...] = xn
    h = xn * (1.0 + m[4:5]) + m[3:4]
    h_hi = h.astype(BF16)
    bits = lax.bitcast_convert_type(h_hi.astype(F32), jnp.uint32)
    half = bits.shape[1] // 2
    ho_ref[...] = bits[:, :half] | (bits[:, half:] >> 16)
    h_lo = (h - h_hi.astype(F32)).astype(BF16)
    w_hi = whi_ref[...]
    logits = (jnp.dot(h_hi, w_hi, preferred_element_type=F32)
              + jnp.dot(h_lo, w_hi, preferred_element_type=F32)
              + jnp.dot(h_hi, wlo_ref[...], preferred_element_type=F32)) + br_ref[...]
    lane = lax.broadcasted_iota(I32, (RT, LANES), 1).astype(F32)
    neg = jnp.float32(-jnp.inf)
    l = jnp.where(lane < n_exp, logits, neg)
    vals, idxs = [], []
    multi = jnp.zeros((RT, LANES), F32)
    for _ in range(TOP_K):
        mx = jnp.max(l, axis=-1, keepdims=True)
        ik = jnp.min(jnp.where(l == mx, lane, float(LANES)), axis=-1, keepdims=True)
        sel = lane == ik
        vals.append(mx)
        idxs.append(ik)
        multi = multi + sel.astype(F32)
        l = jnp.where(sel, neg, l)
    exps = [jnp.exp(v - vals[0]) for v in vals]
    den = exps[0]
    for e in exps[1:]:
        den = den + e
    r_i = lax.broadcasted_iota(I32, (RT, RT), 0)
    c_i = lax.broadcasted_iota(I32, (RT, RT), 1)
    tri = (c_i < r_i).astype(BF16)
    prefix = jnp.dot(tri, multi.astype(BF16), preferred_element_type=F32) + carry_ref[0:1, :]
    ridx = jnp.zeros((RT, LANES), F32)
    rprob = jnp.zeros((RT, LANES), F32)
    for k in range(TOP_K):
        rank = jnp.sum(jnp.where(lane == idxs[k], prefix, 0.0), axis=-1, keepdims=True)
        ridx = jnp.where(lane == k, idxs[k], ridx)
        ridx = jnp.where(lane == TOP_K + k, rank, ridx)
        rprob = jnp.where(lane == k, exps[k] / den, rprob)
    ridx_ref[...] = ridx.astype(I32)
    rprob_ref[...] = rprob
    carry = carry_ref[0:1, :] + jnp.sum(multi, axis=0, keepdims=True)
    carry_ref[...] = jnp.broadcast_to(carry, carry_ref.shape)
    cnt_ref[...] = jnp.broadcast_to(carry, cnt_ref.shape).astype(I32)


def ln_router(cfg, x, f, mod, layer, ln_g, ln_b, w_router, b_router):
    RT, D, T, E = cfg.RT, cfg.D, cfg.T, cfg.E
    wr = jnp.zeros((D, LANES), F32).at[:, :E].set(w_router[layer])
    w_hi = wr.astype(BF16)
    w_lo = (wr - w_hi.astype(F32)).astype(BF16)
    br = jnp.zeros((1, LANES), F32).at[0, :E].set(b_router[layer])
    row = lambda i: (i, 0)
    const = lambda i: (0, 0)
    x_specs, x_args = _split_rows(cfg, x)
    return pl.pallas_call(
        functools.partial(_ln_router_kernel, cfg=cfg),
        grid=(T // RT,),
        in_specs=x_specs + [pl.BlockSpec((RT, D), row), _mod_spec(cfg, layer),
                  pl.BlockSpec((None, None, 1, D), lambda i: (layer, 0, 0, 0)),
                  pl.BlockSpec((None, None, 1, D), lambda i: (layer, 0, 0, 0)),
                  pl.BlockSpec((D, LANES), const), pl.BlockSpec((D, LANES), const),
                  pl.BlockSpec((1, LANES), const)],
        out_specs=[pl.BlockSpec((RT, D), row), pl.BlockSpec((RT, D // 2), row),
                   pl.BlockSpec((RT, LANES), row), pl.BlockSpec((RT, LANES), row),
                   pl.BlockSpec((SUBLANES, LANES), const)],
        out_shape=[jax.ShapeDtypeStruct((T, D), F32), jax.ShapeDtypeStruct((T, D // 2), jnp.uint32),
                   jax.ShapeDtypeStruct((T, LANES), I32), jax.ShapeDtypeStruct((T, LANES), F32),
                   jax.ShapeDtypeStruct((SUBLANES, LANES), I32)],
        scratch_shapes=[pltpu.VMEM((SUBLANES, LANES), F32)],
        compiler_params=_params(("arbitrary",), 48),
        name="ln_router",
    )(*x_args, f, mod, ln_g.reshape(cfg.depth, 2, 1, D), ln_b.reshape(cfg.depth, 2, 1, D), w_hi, w_lo, br)


def _ln_combine_kernel(pos_cur, pos_nxt, x_ref, p_ref, m_ref, g_ref, b_ref, *rest,
                       cfg, n_tiles, has_next):
    if has_next:
        mn_ref, y_hbm, xo_ref, ho_ref, buf, sem = rest
    else:
        y_hbm, xo_ref, xs_ref, buf, sem = rest
    i = pl.program_id(0)
    CT = x_ref.shape[0]
    rows = TOP_K * CT
    slot = i % 2

    def row_copy(src_row, dst_row, s):
        return pltpu.make_async_copy(y_hbm.at[pl.ds(src_row, 1)], buf.at[pl.ds(dst_row, 1)], sem.at[s])

    def wait(s):
        def body(j, c):
            row_copy(0, s * rows, s).wait()
            return c
        lax.fori_loop(0, rows, body, 0, unroll=8)

    @pl.when(i == 0)
    def _():
        def body(j, c):
            row_copy(pos_cur[j], j, 0).start()
            return c
        lax.fori_loop(0, rows, body, 0, unroll=8)

    wait(slot)
    p = p_ref[...]
    base = pl.multiple_of(slot * rows, rows)
    f = jnp.zeros(x_ref.shape, F32)
    for k in range(TOP_K):
        f = f + p[:, k:k + 1] * buf[pl.ds(base + k * CT, CT), :]
    nxt = (1 - slot) * rows
    for j in range(rows):
        row_copy(pos_nxt[j], nxt + j, 1 - slot).start()
    m = m_ref[...]
    xn = _deepnorm_ln(x_ref[...], f, m[5:6], g_ref[...], b_ref[...], cfg.alpha)
    if has_next:
        xo_ref[...] = xn
        mn = mn_ref[...]
        ho_ref[...] = (xn * (1.0 + mn[1:2]) + mn[0:1]).astype(ho_ref.dtype)
    else:
        n_first = cfg.T_p // CT

        @pl.when(i < n_first)
        def _():
            xo_ref[...] = xn

        @pl.when(i >= n_first)
        def _():
            xs_ref[...] = xn

    @pl.when(i == n_tiles - 1)
    def _():
        wait(1 - slot)


def ln_combine(cfg, x, y, pos_flat, rprob, mod, layer, ln_g, ln_b, next_dtype):
    CT, D, T = cfg.CT, cfg.D, cfg.T
    n_tiles = T // CT
    per = cfg.RT // CT
    rows = TOP_K * CT
    has_next = next_dtype is not None
    row = lambda i: (i, 0)
    mod_cur = pl.BlockSpec((None, None, N_MOD, D), lambda i: (layer, cfg.cond_row(i // per), 0, 0))
    in_specs = [pl.BlockSpec((rows,), lambda i: (i,), memory_space=pltpu.SMEM),
                pl.BlockSpec((rows,), lambda i: (jnp.minimum(i + 1, n_tiles - 1),), memory_space=pltpu.SMEM),
                pl.BlockSpec((CT, D), row), pl.BlockSpec((CT, LANES), row), mod_cur,
                pl.BlockSpec((None, None, 1, D), lambda i: (layer, 1, 0, 0)),
                pl.BlockSpec((None, None, 1, D), lambda i: (layer, 1, 0, 0))]
    args = [pos_flat, pos_flat, x, rprob, mod, ln_g.reshape(cfg.depth, 2, 1, D), ln_b.reshape(cfg.depth, 2, 1, D)]
    if has_next:
        in_specs.append(pl.BlockSpec((None, None, N_MOD, D),
                                     lambda i: (layer + 1, cfg.cond_row(i // per), 0, 0)))
        args.append(mod)
        out_specs = [pl.BlockSpec((CT, D), row), pl.BlockSpec((CT, D), row)]
        out_shape = [jax.ShapeDtypeStruct((T, D), F32), jax.ShapeDtypeStruct((T, D), next_dtype)]
    else:
        n_first = cfg.T_p // CT
        out_specs = [pl.BlockSpec((CT, D), lambda i: (jnp.minimum(i, n_first - 1), 0)),
                     pl.BlockSpec((CT, D), lambda i: (jnp.maximum(i - n_first, 0), 0))]
        out_shape = [jax.ShapeDtypeStruct((cfg.T_p, D), F32), jax.ShapeDtypeStruct((cfg.T_s, D), F32)]
    in_specs.append(pl.BlockSpec(memory_space=pl.ANY))
    args.append(y)
    out = pl.pallas_call(
        functools.partial(_ln_combine_kernel, cfg=cfg, n_tiles=n_tiles, has_next=has_next),
        grid=(n_tiles,),
        in_specs=in_specs, out_specs=out_specs, out_shape=out_shape,
        scratch_shapes=[pltpu.VMEM((2 * rows, D), F32), pltpu.SemaphoreType.DMA((2,))],
        compiler_params=_params(("arbitrary",), 48),
        name="ln_combine",
    )(*args)
    return out[0], out[1]


def _mm_kernel(*refs, has_scale, n_first):
    a_refs, refs = (refs[:1], refs[1:]) if n_first is None else (refs[:2], refs[2:])
    if has_scale:
        w_ref, b_ref, s_ref, o_ref, wbf = refs
    else:
        w_ref, b_ref, o_ref, wbf = refs
    i = pl.program_id(2)

    @pl.when(i == 0)
    def _():
        wbf[...] = w_ref[...].astype(BF16)

    def emit(a_ref):
        acc = jnp.dot(a_ref[...], wbf[...], preferred_element_type=F32) + b_ref[...]
        if has_scale:
            acc = acc * s_ref[...]
        o_ref[...] = acc.astype(o_ref.dtype)

    if n_first is None:
        emit(a_refs[0])
    else:
        pl.when(i < n_first)(lambda: emit(a_refs[0]))
        pl.when(i >= n_first)(lambda: emit(a_refs[1]))


def matmul(cfg, a, w4, layer, bias, scale=None, out_dtype=F32, name="matmul"):
    _, G, K, N = w4.shape
    tm, tn = cfg.mm_tile, min(cfg.col_tile, N)
    n_a = 2 if isinstance(a, tuple) else 1
    out_bytes = jnp.dtype(out_dtype).itemsize
    vmem_need = lambda m: (n_a * 2 * m * K * 2 + 2 * K * tn * 4 + K * tn * 2 + 2 * m * tn * out_bytes) / MIB
    while vmem_need(tm) > MM_VMEM_LIMIT_MIB - 4 and tm > SUBLANES:
        tm //= 2
    nj = N // tn
    col = lambda g, j, i: (0, g * nj + j)
    if isinstance(a, tuple):
        n_first = a[0].shape[0] // tm
        M = a[0].shape[0] + a[1].shape[0]
        in_specs = [pl.BlockSpec((tm, K), lambda g, j, i: (jnp.minimum(i, n_first - 1), g)),
                    pl.BlockSpec((tm, K), lambda g, j, i: (jnp.maximum(i - n_first, 0), g))]
        args = list(a)
    else:
        n_first = None
        M = a.shape[0]
        in_specs = [pl.BlockSpec((tm, K), lambda g, j, i: (i, g))]
        args = [a]
    in_specs += [pl.BlockSpec((None, None, K, tn), lambda g, j, i: (layer, g, 0, j)),
                 pl.BlockSpec((1, tn), col)]
    args += [w4, bias]
    if scale is not None:
        in_specs.append(pl.BlockSpec((1, tn), col))
        args.append(scale)
    return pl.pallas_call(
        functools.partial(_mm_kernel, has_scale=scale is not None, n_first=n_first),
        grid=(G, nj, M // tm),
        in_specs=in_specs,
        out_specs=pl.BlockSpec((tm, tn), lambda g, j, i: (i, g * nj + j)),
        out_shape=jax.ShapeDtypeStruct((M, G * N), out_dtype),
        scratch_shapes=[pltpu.VMEM((K, tn), BF16)],
        compiler_params=_params(("parallel", "parallel", "arbitrary"), MM_VMEM_LIMIT_MIB),
        name=name,
    )(*args)


def _hyconv_kernel(*refs, cfg):
    z_refs, w_refs, b_refs = refs[:9], refs[9:12], refs[12:15]
    u_ref, x0_ref = refs[15:17]
    i = pl.program_id(0)
    RT = cfg.RT
    first, last, _ = cfg.seq_edges(i)
    row = lax.broadcasted_iota(I32, u_ref.shape, 0)

    def conv(zc, zp, zn, w, b):
        z = zc[...]
        prev_row = jnp.where(first, 0.0, zp[SUBLANES - 1:SUBLANES, :])
        next_row = jnp.where(last, 0.0, zn[0:1, :])
        z_m = jnp.where(row == 0, prev_row, pltpu.roll(z, 1, 0))
        z_p = jnp.where(row == RT - 1, next_row, pltpu.roll(z, RT - 1, 0))
        wv = w[...]
        return wv[0:1] * z_m + wv[1:2] * z + wv[2:3] * z_p + b[...]

    x0, x1, v = [conv(*z_refs[3 * g:3 * g + 3], w_refs[g], b_refs[g]) for g in range(3)]
    u_ref[...] = v * x1
    x0_ref[...] = x0


def hyena_short_conv(cfg, z, conv_w, conv_b, j):
    RT, D, T, tc = cfg.RT, cfg.D, cfg.T, cfg.ew_tile
    nc = D // tc
    r8 = RT // SUBLANES
    in_specs, args = [], []
    for g in range(3):
        in_specs += [pl.BlockSpec((RT, tc), lambda i, c, g=g: (i, g * nc + c)),
                     pl.BlockSpec((SUBLANES, tc), lambda i, c, g=g: (jnp.maximum(i * r8 - 1, 0), g * nc + c)),
                     pl.BlockSpec((SUBLANES, tc),
                                  lambda i, c, g=g: (jnp.minimum((i + 1) * r8, T // SUBLANES - 1), g * nc + c))]
        args += [z, z, z]
    for g in range(3):
        in_specs.append(pl.BlockSpec((None, SHORT_CONV, tc), lambda i, c, g=g: (j, 0, g * nc + c)))
        args.append(conv_w)
    for g in range(3):
        in_specs.append(pl.BlockSpec((None, 1, tc), lambda i, c, g=g: (j, 0, g * nc + c)))
        args.append(conv_b.reshape(conv_b.shape[0], 1, 3 * D))
    blk = pl.BlockSpec((RT, tc), lambda i, c: (i, c))
    return pl.pallas_call(
        functools.partial(_hyconv_kernel, cfg=cfg),
        grid=(T // RT, nc),
        in_specs=in_specs, out_specs=[blk, blk],
        out_shape=[jax.ShapeDtypeStruct((T, D), F32), jax.ShapeDtypeStruct((T, D), F32)],
        compiler_params=_params(("parallel", "parallel"), 32),
        name="hyena_short_conv",
    )(*args)


def _filter_kernel(z_ref, t_ref, sg_ref, w1_ref, w23_ref, b_ref, fr_ref, wf_ref, wb_ref, dl_ref,
                   x1_ref, x2_ref, kl_ref, hdn_ref):
    @pl.when(pl.program_id(0) == 0)
    def _():
        b = b_ref[...]
        fr = fr_ref[...]
        hdn = jnp.sin(fr[0:1] * (jnp.dot(z_ref[...], w1_ref[...], precision=HIGHEST,
                                         preferred_element_type=F32) + b[0:1]))
        for s in range(2):
            hdn = jnp.sin(fr[s + 1:s + 2] * (jnp.dot(hdn, w23_ref[s], precision=HIGHEST,
                                                     preferred_element_type=F32) + b[s + 1:s + 2]))
        hdn_ref[...] = hdn

    hdn = hdn_ref[...]
    t = t_ref[...]
    win = jnp.exp(-t * dl_ref[...]) + HYENA_DECAY_SHIFT
    hf = jnp.dot(hdn, wf_ref[...], precision=HIGHEST, preferred_element_type=F32) * win
    hb = jnp.dot(hdn, wb_ref[...], precision=HIGHEST, preferred_element_type=F32) * win
    row = lax.broadcasted_iota(I32, hb.shape, 0)
    hb = jnp.where(row == 0, 0.0, hb)
    norm = jnp.sum(jnp.abs(hf), axis=0, keepdims=True) + jnp.sum(jnp.abs(hb), axis=0, keepdims=True)
    x1 = (hf + hb) / norm
    x2 = (hb - hf) / norm
    x1_ref[...] = x1.astype(x1_ref.dtype)
    x2_ref[...] = x2.astype(x2_ref.dtype)
    kl_ref[...] = jnp.sum(sg_ref[...] * x1, axis=0, keepdims=True)


def hyena_position_features(L):
    t = jnp.linspace(0.0, 1.0, L, dtype=F32)[:, None]
    bands = (HYENA_EMB_DIM - 1) // 2
    w = 2.0 * math.pi * jnp.arange(L, dtype=F32)[:, None] / L
    f = jnp.linspace(1e-4, bands - 1, bands, dtype=F32)[None, :]
    return t, jnp.concatenate([t, jnp.cos(f * w), -jnp.sin(f * w)], axis=-1)


def hyena_filter_parts(cfg, L, j, w1, w23, b, freq, w_out):
    D, tc = cfg.D, cfg.filt_tile
    width = w1.shape[-1]
    t, z = hyena_position_features(L)
    zp = jnp.zeros((L, LANES), F32).at[:, :HYENA_EMB_DIM].set(z)
    w1p = jnp.zeros((LANES, width), F32).at[:HYENA_EMB_DIM].set(w1[j])
    sign = (1 - 2 * (jnp.arange(L) % 2)).astype(F32)[:, None]
    min_decay = math.log(HYENA_DECAY_TARGET) / HYENA_SLOW_DECAY
    max_decay = math.log(HYENA_DECAY_TARGET) / HYENA_FAST_DECAY
    deltas = jnp.abs(jnp.linspace(min_decay, max_decay, D, dtype=F32))[None, :]
    nc = D // tc
    const2 = lambda c: (0, 0)
    return pl.pallas_call(
        _filter_kernel,
        grid=(nc,),
        in_specs=[pl.BlockSpec((L, LANES), const2), pl.BlockSpec((L, 1), const2), pl.BlockSpec((L, 1), const2),
                  pl.BlockSpec((LANES, width), const2),
                  pl.BlockSpec((None, 2, width, width), lambda c: (j, 0, 0, 0)),
                  pl.BlockSpec((None, 3, width), lambda c: (j, 0, 0)),
                  pl.BlockSpec((None, 3, width), lambda c: (j, 0, 0)),
                  pl.BlockSpec((None, width, tc), lambda c: (j, 0, c)),
                  pl.BlockSpec((None, width, tc), lambda c: (j, 0, nc + c)),
                  pl.BlockSpec((1, tc), lambda c: (0, c))],
        out_specs=[pl.BlockSpec((L, tc), lambda c: (0, c)), pl.BlockSpec((L, tc), lambda c: (0, c)),
                   pl.BlockSpec((1, tc), lambda c: (0, c))],
        out_shape=[jax.ShapeDtypeStruct((L, D), BF16), jax.ShapeDtypeStruct((L, D), BF16),
                   jax.ShapeDtypeStruct((1, D), F32)],
        scratch_shapes=[pltpu.VMEM((L, width), F32)],
        compiler_params=_params(("arbitrary",), 48),
        name="hyena_filter",
    )(zp, t, sign, w1p, w23, b, freq, w_out, w_out, deltas)


def dft_matrices(L):
    f = jnp.arange(L, dtype=I32)[:, None]
    t = jnp.arange(L, dtype=I32)[None, :]
    ang = ((f * t) % (2 * L)).astype(F32) * (math.pi / L)
    c, s = jnp.cos(ang), jnp.sin(ang)
    sign = (1 - 2 * (t % 2)).astype(F32)
    fwd = jnp.concatenate([c, jnp.where(f == 0, sign, s)], axis=0)
    ga = c * jnp.where(f == 0, 0.5, 1.0) / L
    gb = jnp.where(f == 0, sign / (2 * L), -s / L)
    inv = jnp.concatenate([ga.T, gb.T], axis=1)
    return fwd.astype(BF16), inv.astype(BF16)


def _kf_kernel(f_ref, x1_ref, x2_ref, kl_ref, o_ref, *, nk):
    k2 = pl.program_id(1)

    @pl.when(k2 < nk)
    def _():
        o_ref[...] = jnp.dot(f_ref[...], x1_ref[...], preferred_element_type=F32)

    @pl.when(k2 >= nk)
    def _():
        o_ref[...] = jnp.dot(f_ref[...], x2_ref[...], preferred_element_type=F32)

    @pl.when(k2 == nk)
    def _():
        o_ref[0:1, :] = kl_ref[...]


def hyena_filter_spectrum(cfg, L, fwd, x1, x2, kl):
    D, tc = cfg.D, cfg.col_tile
    tf = min(cfg.freq_tile, L)
    nk = L // tf
    return pl.pallas_call(
        functools.partial(_kf_kernel, nk=nk),
        grid=(D // tc, 2 * nk),
        in_specs=[pl.BlockSpec((tf, L), lambda c, k: (k, 0)),
                  pl.BlockSpec((L, tc), lambda c, k: (0, c)), pl.BlockSpec((L, tc), lambda c, k: (0, c)),
                  pl.BlockSpec((1, tc), lambda c, k: (0, c))],
        out_specs=pl.BlockSpec((tf, tc), lambda c, k: (k, c)),
        out_shape=jax.ShapeDtypeStruct((2 * L, D), F32),
        compiler_params=_params(("parallel", "arbitrary"), 32),
        name="hyena_filter_spectrum",
    )(fwd, x1, x2, kl)


def _lconv_kernel(u_ref, x0_ref, fb_ref, fa_ref, fs_ref, ga_ref, gs_ref, ka_ref, ks_ref, o_ref, acc,
                  *, L, n_seq, nk):
    k = pl.program_id(2)

    @pl.when(k == 0)
    def _():
        acc[...] = jnp.zeros_like(acc)

    ka, ks = ka_ref[...], ks_ref[...]
    row0 = (lax.broadcasted_iota(I32, ka.shape, 0) == 0) & (k == 0)
    for s in range(n_seq):
        rows = pl.ds(s * L, L)
        us = u_ref[rows, :].astype(BF16)
        a = jnp.dot(fa_ref[...], us, preferred_element_type=F32)
        b = jnp.dot(fs_ref[...], us, preferred_element_type=F32)
        bk = b * ks
        ya = a * ka + jnp.where(row0, 0.0, bk)
        ys = jnp.where(row0, bk, a * ks - b * ka)
        acc[rows, :] += (jnp.dot(ga_ref[...], ya.astype(BF16), preferred_element_type=F32)
                         + jnp.dot(gs_ref[...], ys.astype(BF16), preferred_element_type=F32))

    @pl.when(k == nk - 1)
    def _():
        o_ref[...] = ((acc[...] + u_ref[...] * fb_ref[...]) * x0_ref[...]).astype(o_ref.dtype)


def hyena_long_conv(cfg, L, row_off, n_rows, u, x0, filter_bias, j, fwd, inv, kf):
    D, tc = cfg.D, cfg.col_tile
    RB = max(L, cfg.conv_rows)
    n_seq = RB // L
    tf = min(cfg.freq_tile, L)
    nk = L // tf
    off = row_off // RB
    blk = pl.BlockSpec((RB, tc), lambda r, c, k: (off + r, c))
    in_specs = [blk, blk,
                pl.BlockSpec((None, 1, tc), lambda r, c, k: (j, 0, c)),
                pl.BlockSpec((tf, L), lambda r, c, k: (k, 0)), pl.BlockSpec((tf, L), lambda r, c, k: (nk + k, 0)),
                pl.BlockSpec((L, tf), lambda r, c, k: (0, k)), pl.BlockSpec((L, tf), lambda r, c, k: (0, nk + k)),
                pl.BlockSpec((tf, tc), lambda r, c, k: (k, c)), pl.BlockSpec((tf, tc), lambda r, c, k: (nk + k, c))]
    args = [u, x0, filter_bias.reshape(filter_bias.shape[0], 1, D), fwd, fwd, inv, inv, kf, kf]
    return pl.pallas_call(
        functools.partial(_lconv_kernel, L=L, n_seq=n_seq, nk=nk),
        grid=(n_rows // RB, D // tc, nk),
        in_specs=in_specs, out_specs=pl.BlockSpec((RB, tc), lambda r, c, k: (r, c)),
        out_shape=jax.ShapeDtypeStruct((n_rows, D), BF16),
        scratch_shapes=[pltpu.VMEM((RB, tc), F32)],
        compiler_params=_params(("parallel", "parallel", "arbitrary"), 56),
        name="hyena_long_conv",
    )(*args)


def hyena_mixer(cfg, h, j, hy_w_in, hy_b_in, hy_conv_w, hy_conv_b, filt, hy_filter_bias, hy_w_out, hy_b_out):
    D = cfg.D
    z = matmul(cfg, h, hy_w_in.reshape(hy_w_in.shape[0], 1, D, 3 * D), j, hy_b_in[j][None, :], name="hyena_in")
    u, x0 = hyena_short_conv(cfg, z, hy_conv_w, hy_conv_b, j)
    out = []
    for L, row_off, n_rows in ((cfg.seq, 0, cfg.T_p), (cfg.dec_seq, cfg.T_p, cfg.T_s)):
        x1, x2, kl = hyena_filter_parts(cfg, L, j, *filt)
        fwd, inv = dft_matrices(L)
        kf = hyena_filter_spectrum(cfg, L, fwd, x1, x2, kl)
        out.append(hyena_long_conv(cfg, L, row_off, n_rows, u, x0, hy_filter_bias, j, fwd, inv, kf))
    return matmul(cfg, tuple(out), hy_w_out.reshape(hy_w_out.shape[0], 1, D, D), j, hy_b_out[j][None, :], name="hyena_out")


def _qk_post_kernel(x_ref, g_ref, cos_ref, sin_ref, o_ref, *rest, cfg, emit_norm):
    i = pl.program_id(0)
    hd = cfg.hd
    g = g_ref[...]
    lane = lax.broadcasted_iota(I32, (x_ref.shape[0], hd), 1)
    low = (lane % (hd // 2)) < (hd // 4)
    is_latent = i >= cfg.NP_T
    for hh in range(x_ref.shape[1] // hd):
        cols = slice(hh * hd, (hh + 1) * hd)
        x = x_ref[:, cols]
        xn = x * lax.rsqrt(jnp.mean(x * x, axis=-1, keepdims=True) + RMS_EPS) * g
        if emit_norm:
            rest[0][:, cols] = xn
        swapped = jnp.where(low, pltpu.roll(xn, hd - hd // 4, 1), pltpu.roll(xn, hd // 4, 1))
        roped = xn * cos_ref[...] + swapped * sin_ref[...]
        o_ref[:, cols] = jnp.where(is_latent, roped, xn).astype(o_ref.dtype)


def axial_rope_tables(cfg):
    L, hd = cfg.dec_seq, cfg.hd
    axis_dim = hd // 2
    rows = L // cfg.grid_w
    row = jnp.repeat(jnp.arange(rows), cfg.grid_w)
    col = jnp.tile(jnp.arange(cfg.grid_w), rows)
    inv_freq = ROPE_THETA ** (-jnp.arange(0, axis_dim, 2, dtype=F32) / axis_dim)
    ang = jnp.stack([row, col], axis=-1).astype(F32)[..., None] * inv_freq
    cos, sin = jnp.cos(ang), jnp.sin(ang)
    cos_t = jnp.concatenate([cos, cos], axis=-1).reshape(L, hd)
    sin_t = jnp.concatenate([-sin, sin], axis=-1).reshape(L, hd)
    return cos_t, sin_t


def qk_post(cfg, qkv, norm_w, j, cos_t, sin_t, col_off, n_cols, emit_norm):
    RT, T, hd = cfg.RT, cfg.T, cfg.hd
    tc = min(cfg.ew_tile, n_cols)
    blk_off = col_off // tc
    pos_blk = lambda i, c: (jnp.where(i < cfg.NP_T, 0, (i - cfg.NP_T) % cfg.TPS), 0)
    out_specs = [pl.BlockSpec((RT, tc), lambda i, c: (i, c))]
    out_shape = [jax.ShapeDtypeStruct((T, n_cols), BF16)]
    if emit_norm:
        out_specs.append(pl.BlockSpec((RT, tc), lambda i, c: (i, c)))
        out_shape.append(jax.ShapeDtypeStruct((T, n_cols), F32))
    return pl.pallas_call(
        functools.partial(_qk_post_kernel, cfg=cfg, emit_norm=emit_norm),
        grid=(T // RT, n_cols // tc),
        in_specs=[pl.BlockSpec((RT, tc), lambda i, c: (i, blk_off + c)),
                  pl.BlockSpec((None, 1, hd), lambda i, c: (j, 0, 0)),
                  pl.BlockSpec((RT, hd), pos_blk), pl.BlockSpec((RT, hd), pos_blk)],
        out_specs=out_specs, out_shape=out_shape,
        compiler_params=_params(("parallel", "parallel"), 32),
        name="qk_norm_rope",
    )(qkv, norm_w.reshape(norm_w.shape[0], 1, hd), cos_t, sin_t)


def _attn_kernel(*refs, cfg, has_cache):
    if has_cache:
        q_ref, k_ref, v_ref, kc_ref, vc_ref, o_ref = refs
    else:
        q_ref, k_ref, v_ref, o_ref = refs
    hd = cfg.hd
    scale = hd ** -0.5
    nt = (((1,), (1,)), ((), ()))
    k = k_ref[...]
    v = v_ref[...].astype(BF16)
    if has_cache:
        kc = kc_ref[...].astype(BF16)
        vc = vc_ref[...].astype(BF16)
    for g in range(cfg.gqa):
        cols = slice(g * hd, (g + 1) * hd)
        q = q_ref[:, cols]
        s = lax.dot_general(q, k, nt, preferred_element_type=F32) * scale
        m = jnp.max(s, axis=-1, keepdims=True)
        if has_cache:
            sc = lax.dot_general(q, kc, nt, preferred_element_type=F32) * scale
            m = jnp.maximum(m, jnp.max(sc, axis=-1, keepdims=True))
        p = jnp.exp(s - m)
        l = jnp.sum(p, axis=-1, keepdims=True)
        o = jnp.dot(p.astype(BF16), v, preferred_element_type=F32)
        if has_cache:
            pc = jnp.exp(sc - m)
            l = l + jnp.sum(pc, axis=-1, keepdims=True)
            o = o + jnp.dot(pc.astype(BF16), vc, preferred_element_type=F32)
        o_ref[:, cols] = (o / l).astype(o_ref.dtype)


def attention(cfg, q, k, qkv, n_seq, L, row_off, cache):
    RT, D, hd, G = cfg.RT, cfg.D, cfg.hd, cfg.gqa
    v_col = cfg.n_heads + cfg.n_kv
    q_off, kv_off, nq = row_off // RT, row_off // L, L // RT
    in_specs = [pl.BlockSpec((RT, G * hd), lambda b, h, t: (q_off + b * nq + t, h)),
                pl.BlockSpec((L, hd), lambda b, h, t: (kv_off + b, h)),
                pl.BlockSpec((L, hd), lambda b, h, t: (kv_off + b, v_col + h))]
    args = [q, k, qkv]
    if cache is not None:
        ck, cv, n_attn, j = cache
        past = ck.shape[0] // (n_seq * n_attn)
        spec = pl.BlockSpec((past, hd), lambda b, h, t: (b * n_attn + j, h))
        in_specs += [spec, spec]
        args += [ck, cv]
    return pl.pallas_call(
        functools.partial(_attn_kernel, cfg=cfg, has_cache=cache is not None),
        grid=(n_seq, cfg.n_kv, nq),
        in_specs=in_specs,
        out_specs=pl.BlockSpec((RT, G * hd), lambda b, h, t: (b * nq + t, h)),
        out_shape=jax.ShapeDtypeStruct((n_seq * L, D), BF16),
        compiler_params=_params(("parallel", "parallel", "parallel"), 48),
        name="attention",
    )(*args)


def attention_mixer(cfg, h, j, cache_k, cache_v, at_w_qkv, at_q_norm, at_k_norm, at_w_o):
    D, hd = cfg.D, cfg.hd
    n_attn = at_w_qkv.shape[0]
    qkv_dim = at_w_qkv.shape[-1]
    kv_dim = cfg.n_kv * hd
    qkv = matmul(cfg, h, at_w_qkv.reshape(n_attn, 1, D, qkv_dim), j, jnp.zeros((1, qkv_dim), F32), name="attn_qkv")
    cos_t, sin_t = axial_rope_tables(cfg)
    (q,) = qk_post(cfg, qkv, at_q_norm, j, cos_t, sin_t, 0, D, False)
    k, k_norm = qk_post(cfg, qkv, at_k_norm, j, cos_t, sin_t, D, kv_dim, True)
    o_p = attention(cfg, q, k, qkv, cfg.batch, cfg.seq, 0, None)
    ck = cache_k.reshape(-1, kv_dim)
    cv = cache_v.reshape(-1, kv_dim)
    o_s = attention(cfg, q, k, qkv, cfg.dec_batch, cfg.dec_seq, cfg.T_p, (ck, cv, n_attn, j))
    f = matmul(cfg, (o_p, o_s), at_w_o.reshape(n_attn, 1, D, D), j, jnp.zeros((1, D), F32), name="attn_out")
    new_k = k_norm[:cfg.T_p].reshape(cfg.batch, cfg.seq, cfg.n_kv, hd)
    new_v = qkv[:cfg.T_p, D + kv_dim:].reshape(cfg.batch, cfg.seq, cfg.n_kv, hd)
    return f, new_k, new_v


def _pool_kernel(hc, hp, hn, o_ref, ext, *, cfg, cols_per_group):
    i = pl.program_id(0)
    c = pl.program_id(1)
    RT = cfg.RT
    H = SUBLANES
    first, last, p0 = cfg.seq_edges(i)
    L = jnp.where(i < cfg.NP_T, cfg.seq, cfg.dec_seq)
    ext[0:H, :] = jnp.where(first, 0.0, hp[...])
    ext[H:H + RT, :] = hc[...]
    ext[H + RT:, :] = jnp.where(last, 0.0, hn[...])
    pos = p0 + lax.broadcasted_iota(I32, (RT, 1), 0)
    for gi, w in enumerate(POOL_WINDOWS):
        @pl.when(c // cols_per_group == gi)
        def _(w=w):
            half = w // 2
            s = ext[H - half:H - half + RT, :]
            for o in range(1 - half, half):
                s = s + ext[H + o:H + o + RT, :]
            lo = jnp.clip(pos - half, 0, L)
            hi = jnp.clip(pos - half + w, 0, L)
            mean = s / (hi - lo).astype(F32)
            o_ref[...] = (mean - hc[...]).astype(o_ref.dtype)


def pool_mixer(cfg, h, j, pl_w, pl_b, pl_scale):
    RT, D, T = cfg.RT, cfg.D, cfg.T
    gd = D // cfg.n_pool
    tc = min(cfg.ew_tile, gd)
    r8 = RT // SUBLANES
    assert max(POOL_WINDOWS) // 2 <= SUBLANES
    d = pl.pallas_call(
        functools.partial(_pool_kernel, cfg=cfg, cols_per_group=gd // tc),
        grid=(T // RT, D // tc),
        in_specs=[pl.BlockSpec((RT, tc), lambda i, c: (i, c)),
                  pl.BlockSpec((SUBLANES, tc), lambda i, c: (jnp.maximum(i * r8 - 1, 0), c)),
                  pl.BlockSpec((SUBLANES, tc), lambda i, c: (jnp.minimum((i + 1) * r8, T // SUBLANES - 1), c))],
        out_specs=pl.BlockSpec((RT, tc), lambda i, c: (i, c)),
        out_shape=jax.ShapeDtypeStruct((T, D), BF16),
        scratch_shapes=[pltpu.VMEM((RT + 2 * SUBLANES, tc), F32)],
        compiler_params=_params(("parallel", "parallel"), 32),
        name="pool_window",
    )(h, h, h)
    return matmul(cfg, d, pl_w, j, pl_b[j][None, :], scale=pl_scale[j][None, :], name="pool_proj")


def _cast_kernel(x_ref, o_ref):
    o_ref[...] = x_ref[...].astype(o_ref.dtype)


def cast_bf16(w, rows=1024):
    shape = w.shape
    w2 = w.reshape(-1, shape[-1])
    rows = min(rows, w2.shape[0])
    out = pl.pallas_call(
        _cast_kernel,
        grid=(w2.shape[0] // rows,),
        in_specs=[pl.BlockSpec((rows, shape[-1]), lambda i: (i, 0))],
        out_specs=pl.BlockSpec((rows, shape[-1]), lambda i: (i, 0)),
        out_shape=jax.ShapeDtypeStruct(w2.shape, BF16),
        compiler_params=_params(("parallel",), 48),
        name="cast_bf16",
    )(w2)
    return out.reshape(shape)


def _moe_up_kernel(te_ref, nu_ref, src_cur, src_nxt, h_hbm, w_ref, b_ref, o_ref, buf, sem, *, F):
    t = pl.program_id(0)
    n_used = nu_ref[0]
    TM = o_ref.shape[0]
    slot = t % 2

    def row_copy(src_row, dst_row, s):
        return pltpu.make_async_copy(h_hbm.at[pl.ds(src_row, 1)], buf.at[pl.ds(dst_row, 1)], sem.at[s])

    def issue(src_ref, s):
        def body(r, c):
            row_copy(src_ref[r], s * TM + r, s).start()
            return c
        lax.fori_loop(0, TM, body, 0, unroll=8)

    def wait(s):
        def body(r, c):
            row_copy(0, s * TM, s).wait()
            return c
        lax.fori_loop(0, TM, body, 0, unroll=8)

    @pl.when((t == 0) & (n_used > 0))
    def _():
        issue(src_cur, 0)

    @pl.when(t + 1 < n_used)
    def _():
        issue(src_nxt, 1 - slot)

    @pl.when(t < n_used)
    def _():
        wait(slot)
        words = buf[pl.ds(pl.multiple_of(slot * TM, TM), TM), :]
        first = lax.bitcast_convert_type(words & jnp.uint32(0xFFFF0000), F32).astype(BF16)
        second = lax.bitcast_convert_type(words << 16, F32).astype(BF16)
        x = jnp.concatenate([first, second], axis=1)
        gu = jnp.dot(x, w_ref[...], preferred_element_type=F32) + b_ref[...]
        g = jnp.minimum(gu[:, :F], SWIGLU_LIMIT)
        u = jnp.clip(gu[:, F:], -SWIGLU_LIMIT, SWIGLU_LIMIT)
        o_ref[...] = (g * jax.nn.sigmoid(SWIGLU_ALPHA * g) * (u + 1.0)).astype(o_ref.dtype)

    @pl.when(t >= n_used)
    def _():
        o_ref[...] = jnp.zeros_like(o_ref)


def moe_up(cfg, h, src, tile_expert, n_used, w_gu_bf, b_gu, layer):
    TM, D, F, NT = cfg.TM, cfg.D, cfg.F, cfg.NT
    grid_spec = pltpu.PrefetchScalarGridSpec(
        num_scalar_prefetch=2,
        grid=(NT,),
        in_specs=[pl.BlockSpec((TM,), lambda t, te, nu: (t,), memory_space=pltpu.SMEM),
                  pl.BlockSpec((TM,), lambda t, te, nu: (jnp.minimum(t + 1, NT - 1),), memory_space=pltpu.SMEM),
                  pl.BlockSpec(memory_space=pl.ANY),
                  pl.BlockSpec((None, None, D, 2 * F), lambda t, te, nu: (layer, te[t], 0, 0)),
                  pl.BlockSpec((None, None, 1, 2 * F), lambda t, te, nu: (layer, te[t], 0, 0))],
        out_specs=pl.BlockSpec((TM, F), lambda t, te, nu: (t, 0)),
        scratch_shapes=[pltpu.VMEM((2 * TM, D // 2), jnp.uint32), pltpu.SemaphoreType.DMA((2,))])
    return pl.pallas_call(
        functools.partial(_moe_up_kernel, F=F),
        grid_spec=grid_spec,
        out_shape=jax.ShapeDtypeStruct((cfg.P, F), BF16),
        compiler_params=_params(("arbitrary",), 56),
        name="moe_up",
    )(tile_expert, n_used, src, src, h, w_gu_bf, b_gu.reshape(cfg.depth, cfg.E, 1, 2 * F))


def _moe_down_kernel(te_ref, nu_ref, a_ref, w_ref, b_ref, o_ref, wbf):
    t = pl.program_id(0)

    @pl.when((t == 0) | (te_ref[t] != te_ref[jnp.maximum(t - 1, 0)]))
    def _():
        wbf[...] = w_ref[...].astype(BF16)

    @pl.when(t < nu_ref[0])
    def _():
        o_ref[...] = jnp.dot(a_ref[...], wbf[...], preferred_element_type=F32) + b_ref[...]

    @pl.when(t >= nu_ref[0])
    def _():
        o_ref[...] = jnp.zeros_like(o_ref)


def moe_down(cfg, a, tile_expert, n_used, w_down, b_down, layer):
    TM, D, F, NT = cfg.TM, cfg.D, cfg.F, cfg.NT
    grid_spec = pltpu.PrefetchScalarGridSpec(
        num_scalar_prefetch=2,
        grid=(NT,),
        in_specs=[pl.BlockSpec((TM, F), lambda t, te, nu: (t, 0)),
                  pl.BlockSpec((None, None, F, D), lambda t, te, nu: (layer, te[t], 0, 0)),
                  pl.BlockSpec((None, None, 1, D), lambda t, te, nu: (layer, te[t], 0, 0))],
        out_specs=pl.BlockSpec((TM, D), lambda t, te, nu: (t, 0)),
        scratch_shapes=[pltpu.VMEM((F, D), BF16)])
    return pl.pallas_call(
        _moe_down_kernel,
        grid_spec=grid_spec,
        out_shape=jax.ShapeDtypeStruct((cfg.P, D), F32),
        compiler_params=_params(("arbitrary",), 56),
        name="moe_down",
    )(tile_expert, n_used, a, w_down, b_down.reshape(cfg.depth, cfg.E, 1, D))


def route_metadata(cfg, ridx, cnt):
    T, E, TM, NT, CT = cfg.T, cfg.E, cfg.TM, cfg.NT, cfg.CT
    idx = ridx[:, :TOP_K]
    rank = ridx[:, TOP_K:2 * TOP_K]
    counts = cnt[0, :E]
    padded = ((counts + TM - 1) // TM) * TM
    e_i = jnp.arange(E, dtype=I32)
    ends = jnp.sum(jnp.where(e_i[None, :] <= e_i[:, None], padded[None, :], 0), axis=1)
    starts = ends - padded
    pos = rank + jnp.sum(jnp.where(idx[:, :, None] == e_i, starts, 0), axis=-1)
    token = jnp.broadcast_to(jnp.arange(T, dtype=I32)[:, None], (T, TOP_K))
    src = jnp.zeros((cfg.P,), I32).at[pos.reshape(-1)].set(token.reshape(-1), unique_indices=True)
    n_used = ends[-1] // TM
    tile_start = jnp.minimum(jnp.arange(NT, dtype=I32), n_used - 1) * TM
    te = jnp.sum((ends[None, :] <= tile_start[:, None]).astype(I32), axis=1)
    te = jnp.minimum(te, E - 1)
    pos_flat = pos.reshape(T // CT, CT, TOP_K).transpose(0, 2, 1).reshape(-1)
    return src, te, n_used.reshape(1), pos_flat


def kernel(x_prompt, x_sample, cache_k, cache_v, c, c_ctx, mod_w, mod_b, ln_g, ln_b, hy_w_in, hy_b_in, hy_conv_w, hy_conv_b, hy_ffn_w1, hy_ffn_w23, hy_ffn_b, hy_sin_freq, hy_ffn_w_out, hy_filter_bias, hy_w_out, hy_b_out, at_w_qkv, at_q_norm, at_k_norm, at_w_o, pl_w, pl_b, pl_scale, moe_w_router, moe_b_router, moe_w_gu, moe_b_gu, moe_w_down, moe_b_down):
    batch, seq, D = x_prompt.shape
    dec_batch, dec_seq, _ = x_sample.shape
    depth = mod_w.shape[0]
    cfg = Cfg(D, batch, seq, dec_batch, dec_seq, cache_k.shape[2], depth,
              moe_w_router.shape[-1], moe_w_down.shape[2])
    return trunk_step(cfg, x_prompt, x_sample, cache_k, cache_v, c, c_ctx, mod_w, mod_b, ln_g, ln_b,
                      hy_w_in, hy_b_in, hy_conv_w, hy_conv_b, hy_ffn_w1, hy_ffn_w23, hy_ffn_b, hy_sin_freq,
                      hy_ffn_w_out, hy_filter_bias, hy_w_out, hy_b_out, at_w_qkv, at_q_norm, at_k_norm, at_w_o,
                      pl_w, pl_b, pl_scale, moe_w_router, moe_b_router, moe_w_gu, moe_b_gu, moe_w_down,
                      moe_b_down)


def trunk_step(cfg, x_prompt, x_sample, cache_k, cache_v, c, c_ctx, mod_w, mod_b, ln_g, ln_b,
               hy_w_in, hy_b_in, hy_conv_w, hy_conv_b, hy_ffn_w1, hy_ffn_w23, hy_ffn_b, hy_sin_freq,
               hy_ffn_w_out, hy_filter_bias, hy_w_out, hy_b_out, at_w_qkv, at_q_norm, at_k_norm, at_w_o,
               pl_w, pl_b, pl_scale, moe_w_router, moe_b_router, moe_w_gu, moe_b_gu, moe_w_down, moe_b_down):
    D, depth = cfg.D, cfg.depth
    n_mixers = 3
    assert cfg.dec_batch + 1 <= COND_ROWS
    cond = jnp.zeros((COND_ROWS, D), F32).at[0].set(c_ctx).at[1:1 + cfg.dec_batch].set(c)
    mod = modulation_all(cfg, cond, mod_w, mod_b)
    w_gu_bf = cast_bf16(moe_w_gu)
    x = (x_prompt.reshape(cfg.T_p, D), x_sample.reshape(cfg.T_s, D))
    mixer_in_dtype = lambda i: F32 if i % n_mixers == 2 else BF16
    h = modulate_first(cfg, x, mod, 0, mixer_in_dtype(0))
    ctx_k, ctx_v = [], []
    for i in range(depth):
        kind, j = i % n_mixers, i // n_mixers
        if kind == 0:
            filt = (hy_ffn_w1, hy_ffn_w23, hy_ffn_b, hy_sin_freq, hy_ffn_w_out)
            f = hyena_mixer(cfg, h, j, hy_w_in, hy_b_in, hy_conv_w, hy_conv_b, filt,
                            hy_filter_bias, hy_w_out, hy_b_out)
        elif kind == 1:
            f, new_k, new_v = attention_mixer(cfg, h, j, cache_k, cache_v, at_w_qkv, at_q_norm, at_k_norm, at_w_o)
            ctx_k.append(new_k)
            ctx_v.append(new_v)
        else:
            f = pool_mixer(cfg, h, j, pl_w, pl_b, pl_scale)
        x, h_ffn, ridx, rprob, cnt = ln_router(cfg, x, f, mod, i, ln_g, ln_b, moe_w_router, moe_b_router)
        src, tile_expert, n_used, pos_flat = route_metadata(cfg, ridx, cnt)
        a = moe_up(cfg, h_ffn, src, tile_expert, n_used, w_gu_bf, moe_b_gu, i)
        y = moe_down(cfg, a, tile_expert, n_used, moe_w_down, moe_b_down, i)
        if i + 1 < depth:
            x, h = ln_combine(cfg, x, y, pos_flat, rprob, mod, i, ln_g, ln_b, mixer_in_dtype(i + 1))
        else:
            xp, xs = ln_combine(cfg, x, y, pos_flat, rprob, mod, i, ln_g, ln_b, None)
    xp = xp.reshape(cfg.batch, cfg.seq, D)
    xs = xs.reshape(cfg.dec_batch, cfg.dec_seq, D)
    return (xp, xs, jnp.stack(ctx_k, axis=1), jnp.stack(ctx_v, axis=1))
```
